```python
import math
import jax, jax.numpy as jnp
from jax import lax
import numpy as np

D_MODEL = 4096
BATCH = 2
SEQ = 4096
DEPTH = 2

HEAD_DIM = 128
DIFF_V_DIM = 2 * HEAD_DIM
DIFF_HEADS = D_MODEL // (4 * DIFF_V_DIM)
MOBA_HEADS = (D_MODEL - DIFF_HEADS * DIFF_V_DIM) // (2 * HEAD_DIM)
FOX_HEADS = MOBA_HEADS
MOBA_WIDTH = MOBA_HEADS * HEAD_DIM
FOX_WIDTH = FOX_HEADS * HEAD_DIM
DIFF_QK_WIDTH = DIFF_HEADS * 2 * HEAD_DIM
DIFF_WIDTH = DIFF_HEADS * DIFF_V_DIM
MIX_WIDTH = MOBA_WIDTH + FOX_WIDTH + DIFF_WIDTH
IN_SIZES = [MOBA_WIDTH] * 3 + [FOX_WIDTH] * 3 + [FOX_HEADS] + [DIFF_QK_WIDTH] * 2 + [DIFF_WIDTH]
IN_WIDTH = sum(IN_SIZES)
FFN_DIM = -(-8 * D_MODEL // (3 * 256)) * 256
MOBA_BLOCK = 256
MOBA_TOPK = 3
MOBA_QCHUNK = 32
DENSE_QBLOCK = 128
ROPE_THETA = 10000.0
NORM_EPS = 1e-6
NEG_INF = -1e30

kernel_name = "hybrid_moba_fox_diff_parallel_heads"


def rms_norm(x, g):
    xf = x.astype(jnp.float32)
    y = xf * lax.rsqrt(jnp.mean(xf * xf, axis=-1, keepdims=True) + NORM_EPS)
    return (y * g.astype(jnp.float32)).astype(x.dtype)


def rotary_tables(seq):
    inv_freq = 1.0 / (ROPE_THETA ** (jnp.arange(0, HEAD_DIM, 2, dtype=jnp.float32) / HEAD_DIM))
    ang = jnp.arange(seq, dtype=jnp.float32)[:, None] * inv_freq[None, :]
    return jnp.cos(ang), jnp.sin(ang)


def apply_rotary(x, cos, sin):
    half = x.shape[-1] // 2
    shape = (x.shape[1],) + (1,) * (x.ndim - 3) + (half,)
    c, s = cos.reshape(shape), sin.reshape(shape)
    xf = x.astype(jnp.float32)
    x1, x2 = xf[..., :half], xf[..., half:]
    return jnp.concatenate([x1 * c - x2 * s, x2 * c + x1 * s], axis=-1).astype(x.dtype)


def _causal_mask(start, end):
    q_pos = jnp.arange(start, end)
    k_pos = jnp.arange(end)
    return k_pos[None, :] <= q_pos[:, None]


def _sweep_query_blocks(block_fn, seq):
    return jnp.concatenate([block_fn(s, s + DENSE_QBLOCK) for s in range(0, seq, DENSE_QBLOCK)], axis=1)


def moba_attention(q, k, v):
    B, S, H, D = q.shape
    nb = -(-S // MOBA_BLOCK)
    s_pad = nb * MOBA_BLOCK
    if s_pad != S:
        pw = ((0, 0), (0, s_pad - S), (0, 0), (0, 0))
        q, k, v = jnp.pad(q, pw), jnp.pad(k, pw), jnp.pad(v, pw)
    scale = D ** -0.5
    qh = q.transpose(0, 2, 1, 3)
    kb = k.transpose(0, 2, 1, 3).reshape(B, H, nb, MOBA_BLOCK, D)
    vb = v.transpose(0, 2, 1, 3).reshape(B, H, nb, MOBA_BLOCK, D)
    topk = min(MOBA_TOPK, nb - 1)
    if topk > 0:
        k_mean = jnp.mean(kb.astype(jnp.float32), axis=3)
        gate = jnp.einsum('bhsd,bhnd->bhsn', qh.astype(jnp.float32), k_mean)
        q_block = jnp.arange(s_pad) // MOBA_BLOCK
        fully_past = jnp.arange(nb)[None, :] < q_block[:, None]
        gate = jnp.where(fully_past, gate, NEG_INF)
        _, sel = lax.top_k(gate, topk)
        sel_valid = sel < q_block[None, None, :, None]
    gather_blocks = jax.vmap(jax.vmap(lambda blocks, idx: blocks[idx]))

    def chunk(i):
        start = i * MOBA_QCHUNK
        blk = start // MOBA_BLOCK
        q_c = lax.dynamic_slice_in_dim(qh, start, MOBA_QCHUNK, axis=2)
        k_own = lax.dynamic_index_in_dim(kb, blk, axis=2, keepdims=False)
        v_own = lax.dynamic_index_in_dim(vb, blk, axis=2, keepdims=False)
        q_pos = start + jnp.arange(MOBA_QCHUNK)
        k_pos = blk * MOBA_BLOCK + jnp.arange(MOBA_BLOCK)
        s_own = jnp.einsum('bhcd,bhkd->bhck', q_c, k_own).astype(jnp.float32) * scale
        s_own = jnp.where(k_pos[None, :] <= q_pos[:, None], s_own, NEG_INF)
        if topk == 0:
            p = jax.nn.softmax(s_own, axis=-1).astype(v.dtype)
            return jnp.einsum('bhck,bhkd->bhcd', p, v_own)
        sel_c = lax.dynamic_slice_in_dim(sel, start, MOBA_QCHUNK, axis=2)
        valid_c = lax.dynamic_slice_in_dim(sel_valid, start, MOBA_QCHUNK, axis=2)
        k_sel = gather_blocks(kb, sel_c)
        v_sel = gather_blocks(vb, sel_c)
        s_sel = jnp.einsum('bhcd,bhcjkd->bhcjk', q_c, k_sel).astype(jnp.float32) * scale
        s_sel = jnp.where(valid_c[..., None], s_sel, NEG_INF)
        s_all = jnp.concatenate([s_sel.reshape(B, H, MOBA_QCHUNK, topk * MOBA_BLOCK), s_own], axis=-1)
        p = jax.nn.softmax(s_all, axis=-1).astype(v.dtype)
        p_sel = p[..., :topk * MOBA_BLOCK].reshape(B, H, MOBA_QCHUNK, topk, MOBA_BLOCK)
        p_own = p[..., topk * MOBA_BLOCK:]
        return (jnp.einsum('bhcjk,bhcjkd->bhcd', p_sel, v_sel)
                + jnp.einsum('bhck,bhkd->bhcd', p_own, v_own))

    outs = lax.map(chunk, jnp.arange(s_pad // MOBA_QCHUNK))
    out = outs.transpose(1, 0, 3, 2, 4).reshape(B, s_pad, H, D)
    return out[:, :S]


def forgetting_attention(q, k, v, log_f):
    S, D = q.shape[1], q.shape[-1]
    scale = D ** -0.5
    cum = jnp.cumsum(log_f.astype(jnp.float32), axis=1).transpose(0, 2, 1)

    def block(start, end):
        s = jnp.einsum('bqhd,bkhd->bhqk', q[:, start:end], k[:, :end]).astype(jnp.float32) * scale
        s = s + cum[:, :, start:end, None] - cum[:, :, None, :end]
        s = jnp.where(_causal_mask(start, end), s, NEG_INF)
        p = jax.nn.softmax(s, axis=-1).astype(v.dtype)
        return jnp.einsum('bhqk,bkhd->bqhd', p, v[:, :end])

    return _sweep_query_blocks(block, S)


def differential_attention(q, k, v, lam):
    S, D = q.shape[1], q.shape[-1]
    scale = D ** -0.5

    def block(start, end):
        s = jnp.einsum('bqhmd,bkhmd->bhmqk', q[:, start:end], k[:, :end]).astype(jnp.float32) * scale
        s = jnp.where(_causal_mask(start, end), s, NEG_INF)
        p = jax.nn.softmax(s, axis=-1)
        p = (p[:, :, 0] - lam * p[:, :, 1]).astype(v.dtype)
        return jnp.einsum('bhqk,bkhe->bqhe', p, v[:, :end])

    return _sweep_query_blocks(block, S)


def setup_inputs(seed: int = 0) -> dict:
    key = jax.random.key(seed)
    ks = jax.random.split(key, 24)
    f32 = jnp.float32
    nrm = lambda k, shape, scale: jax.random.normal(k, shape, f32) * scale
    gain = lambda k, shape: 1.0 + 0.02 * jax.random.normal(k, shape, f32)
    return {
        "x": nrm(ks[0], (BATCH, SEQ, D_MODEL), 1.0),
        "attn_norm": gain(ks[1], (DEPTH, D_MODEL)),
        "w_in": nrm(ks[2], (DEPTH, D_MODEL, IN_WIDTH), D_MODEL ** -0.5),
        "moba_q_norm": gain(ks[3], (DEPTH, HEAD_DIM)),
        "moba_k_norm": gain(ks[4], (DEPTH, HEAD_DIM)),
        "fox_q_norm": gain(ks[5], (DEPTH, HEAD_DIM)),
        "fox_k_norm": gain(ks[6], (DEPTH, HEAD_DIM)),
        "fox_forget_bias": jax.random.uniform(ks[7], (DEPTH, FOX_HEADS), f32, 1.0, 4.0),
        "diff_q_norm": gain(ks[8], (DEPTH, HEAD_DIM)),
        "diff_k_norm": gain(ks[9], (DEPTH, HEAD_DIM)),
        "diff_lambda_q1": nrm(ks[10], (DEPTH, HEAD_DIM), 0.1),
        "diff_lambda_k1": nrm(ks[11], (DEPTH, HEAD_DIM), 0.1),
        "diff_lambda_q2": nrm(ks[12], (DEPTH, HEAD_DIM), 0.1),
        "diff_lambda_k2": nrm(ks[13], (DEPTH, HEAD_DIM), 0.1),
        "diff_sub_norm": gain(ks[14], (DEPTH, DIFF_V_DIM)),
        "w_out": nrm(ks[15], (DEPTH, MIX_WIDTH, D_MODEL), MIX_WIDTH ** -0.5),
        "ffn_norm": gain(ks[16], (DEPTH, D_MODEL)),
        "w_gate": nrm(ks[17], (DEPTH, D_MODEL, FFN_DIM), D_MODEL ** -0.5),
        "w_up": nrm(ks[18], (DEPTH, D_MODEL, FFN_DIM), D_MODEL ** -0.5),
        "w_down": nrm(ks[19], (DEPTH, FFN_DIM, D_MODEL), FFN_DIM ** -0.5),
    }


def reference(x, attn_norm, w_in, moba_q_norm, moba_k_norm, fox_q_norm, fox_k_norm,
              fox_forget_bias, diff_q_norm, diff_k_norm, diff_lambda_q1, diff_lambda_k1,
              diff_lambda_q2, diff_lambda_k2, diff_sub_norm, w_out, ffn_norm, w_gate, w_up, w_down):
    B, S, _ = x.shape
    cos, sin = rotary_tables(S)
    split_idx = np.cumsum(IN_SIZES)[:-1].tolist()
    for l in range(DEPTH):
        h = rms_norm(x, attn_norm[l])
        proj = jnp.einsum('bsd,de->bse', h, w_in[l])
        q_m, k_m, v_m, q_f, k_f, v_f, f_logit, q_d, k_d, v_d = jnp.split(proj, split_idx, axis=-1)

        q_m = apply_rotary(rms_norm(q_m.reshape(B, S, MOBA_HEADS, HEAD_DIM), moba_q_norm[l]), cos, sin)
        k_m = apply_rotary(rms_norm(k_m.reshape(B, S, MOBA_HEADS, HEAD_DIM), moba_k_norm[l]), cos, sin)
        o_a = moba_attention(q_m, k_m, v_m.reshape(B, S, MOBA_HEADS, HEAD_DIM))

        q_f = rms_norm(q_f.reshape(B, S, FOX_HEADS, HEAD_DIM), fox_q_norm[l])
        k_f = rms_norm(k_f.reshape(B, S, FOX_HEADS, HEAD_DIM), fox_k_norm[l])
        log_f = jax.nn.log_sigmoid(f_logit.astype(jnp.float32) + fox_forget_bias[l].astype(jnp.float32))
        o_b = forgetting_attention(q_f, k_f, v_f.reshape(B, S, FOX_HEADS, HEAD_DIM), log_f)

        lam_init = 0.8 - 0.6 * math.exp(-0.3 * l)
        lam = (jnp.exp(jnp.sum(diff_lambda_q1[l].astype(jnp.float32) * diff_lambda_k1[l].astype(jnp.float32)))
               - jnp.exp(jnp.sum(diff_lambda_q2[l].astype(jnp.float32) * diff_lambda_k2[l].astype(jnp.float32)))
               + lam_init)
        q_d = apply_rotary(rms_norm(q_d.reshape(B, S, DIFF_HEADS, 2, HEAD_DIM), diff_q_norm[l]), cos, sin)
        k_d = apply_rotary(rms_norm(k_d.reshape(B, S, DIFF_HEADS, 2, HEAD_DIM), diff_k_norm[l]), cos, sin)
        o_c = differential_attention(q_d, k_d, v_d.reshape(B, S, DIFF_HEADS, DIFF_V_DIM), lam)
        o_c = rms_norm(o_c, diff_sub_norm[l]) * (1.0 - lam_init)

        mixed = jnp.concatenate([o_a.reshape(B, S, MOBA_WIDTH), o_b.reshape(B, S, FOX_WIDTH),
                                 o_c.reshape(B, S, DIFF_WIDTH)], axis=-1)
        x = x + jnp.einsum('bse,ed->bsd', mixed, w_out[l])

        h = rms_norm(x, ffn_norm[l])
        g = jnp.einsum('bsd,df->bsf', h, w_gate[l])
        u = jnp.einsum('bsd,df->bsf', h, w_up[l])
        x = x + jnp.einsum('bsf,fd->bsd', jax.nn.silu(g) * u, w_down[l])
    return x
```

```python
import functools
import math

import jax
import jax.numpy as jnp
from jax import lax
from jax.experimental import pallas as pl
from jax.experimental.pallas import tpu as pltpu

D_MODEL = 4096
HEAD_DIM = 128
DIFF_V_DIM = 2 * HEAD_DIM
DIFF_HEADS = D_MODEL // (4 * DIFF_V_DIM)
MOBA_HEADS = (D_MODEL - DIFF_HEADS * DIFF_V_DIM) // (2 * HEAD_DIM)
FOX_HEADS = MOBA_HEADS
MOBA_WIDTH = MOBA_HEADS * HEAD_DIM
FOX_WIDTH = FOX_HEADS * HEAD_DIM
DIFF_QK_WIDTH = DIFF_HEADS * 2 * HEAD_DIM
DIFF_WIDTH = DIFF_HEADS * DIFF_V_DIM
FFN_DIM = -(-8 * D_MODEL // (3 * 256)) * 256
MOBA_BLOCK = 256
MOBA_TOPK = 3
ROPE_THETA = 10000.0
NORM_EPS = 1e-6
NEG_INF = -1e30

QM_OFF = 0
KM_OFF = QM_OFF + MOBA_WIDTH
VM_OFF = KM_OFF + MOBA_WIDTH
QF_OFF = VM_OFF + MOBA_WIDTH
KF_OFF = QF_OFF + FOX_WIDTH
VF_OFF = KF_OFF + FOX_WIDTH
FLOGIT_OFF = VF_OFF + FOX_WIDTH
QD_OFF = VF_OFF + FOX_WIDTH
KD_OFF = QD_OFF + DIFF_QK_WIDTH
VD_OFF = KD_OFF + DIFF_QK_WIDTH
PROJ_WIDTH = VD_OFF + DIFF_WIDTH

LANES = 128
FFN_PAD = 11264
VMEM_LIMIT = 56 * 1024 * 1024

IN_TN = 512
ATT_T = 256

F32 = jnp.float32
BF16 = jnp.bfloat16


def _params(sem):
    return pltpu.CompilerParams(dimension_semantics=sem, vmem_limit_bytes=VMEM_LIMIT)


def _rmsnorm_kernel(x_ref, g_ref, o_ref):
    x = x_ref[...]
    ms = jnp.mean(x * x, axis=-1, keepdims=True)
    o_ref[...] = (x * lax.rsqrt(ms + NORM_EPS) * g_ref[...]).astype(o_ref.dtype)


def _rmsnorm(x, g, tm=256):
    m, d = x.shape
    return pl.pallas_call(
        _rmsnorm_kernel,
        grid=(m // tm,),
        in_specs=[pl.BlockSpec((tm, d), lambda i: (i, 0)),
                  pl.BlockSpec((1, d), lambda i: (0, 0))],
        out_specs=pl.BlockSpec((tm, d), lambda i: (i, 0)),
        out_shape=jax.ShapeDtypeStruct((m, d), BF16),
        compiler_params=_params(("parallel",)),
        name="rmsnorm",
    )(x, g.reshape(1, d))


def _tile_kind(n):
    def in_seg(off, width):
        return (n >= off // IN_TN) & (n < (off + width) // IN_TN)
    rot = (in_seg(QM_OFF, 2 * MOBA_WIDTH) | in_seg(QD_OFF, 2 * DIFF_QK_WIDTH))
    norm = in_seg(QF_OFF, 2 * FOX_WIDTH)
    return rot, norm


def _inproj_kernel(h_ref, w_ref, wf_ref, gain_ref, cos_ref, sin_ref, o_ref, flog_ref, acc_ref):
    n = pl.program_id(1)
    acc_ref[...] = jnp.dot(h_ref[...], w_ref[...], preferred_element_type=F32)

    @pl.when(n == 0)
    def _():
        flog_ref[...] = jnp.dot(h_ref[...], wf_ref[...], preferred_element_type=F32)

    rot, norm = _tile_kind(n)

    def normed(hh):
        a = acc_ref[:, hh * HEAD_DIM:(hh + 1) * HEAD_DIM]
        ms = jnp.mean(a * a, axis=-1, keepdims=True)
        return a * lax.rsqrt(ms + NORM_EPS) * gain_ref[:, hh * HEAD_DIM:(hh + 1) * HEAD_DIM]

    @pl.when(rot)
    def _():
        for hh in range(IN_TN // HEAD_DIM):
            y = normed(hh)
            y = y * cos_ref[...] + pltpu.roll(y, HEAD_DIM // 2, 1) * sin_ref[...]
            o_ref[:, hh * HEAD_DIM:(hh + 1) * HEAD_DIM] = y.astype(o_ref.dtype)

    @pl.when(norm)
    def _():
        for hh in range(IN_TN // HEAD_DIM):
            o_ref[:, hh * HEAD_DIM:(hh + 1) * HEAD_DIM] = normed(hh).astype(o_ref.dtype)

    @pl.when(jnp.logical_not(rot | norm))
    def _():
        o_ref[...] = acc_ref[...].astype(o_ref.dtype)


def _inproj(h, w, wf, gain, cos_full, sin_signed, seq, tm=1024):
    m, d = h.shape
    n_tiles = PROJ_WIDTH // IN_TN
    pos_tiles = seq // tm
    return pl.pallas_call(
        _inproj_kernel,
        grid=(m // tm, n_tiles),
        in_specs=[pl.BlockSpec((tm, d), lambda i, j: (i, 0)),
                  pl.BlockSpec((d, IN_TN), lambda i, j: (0, j)),
                  pl.BlockSpec((d, LANES), lambda i, j: (0, 0)),
                  pl.BlockSpec((1, IN_TN), lambda i, j: (0, j)),
                  pl.BlockSpec((tm, HEAD_DIM), lambda i, j: (i % pos_tiles, 0)),
                  pl.BlockSpec((tm, HEAD_DIM), lambda i, j: (i % pos_tiles, 0))],
        out_specs=[pl.BlockSpec((tm, IN_TN), lambda i, j: (i, j)),
                   pl.BlockSpec((tm, LANES), lambda i, j: (i, 0))],
        out_shape=[jax.ShapeDtypeStruct((m, PROJ_WIDTH), BF16),
                   jax.ShapeDtypeStruct((m, LANES), F32)],
        scratch_shapes=[pltpu.VMEM((tm, IN_TN), F32)],
        compiler_params=_params(("parallel", "arbitrary")),
        name="inproj",
    )(h, w, wf, gain, cos_full, sin_signed)


def _fox_prep_kernel(fl_ref, b_ref, o_ref, *, chunk):
    seq = fl_ref.shape[0]
    row = lax.broadcasted_iota(jnp.int32, (chunk, chunk), 0)
    col = lax.broadcasted_iota(jnp.int32, (chunk, chunk), 1)
    tri = jnp.where(row >= col, 1.0, 0.0).astype(F32)

    def body(i, carry):
        start = pl.multiple_of(i * chunk, chunk)
        z = fl_ref[pl.ds(start, chunk), :] + b_ref[...]
        log_f = -(jnp.maximum(-z, 0.0) + jnp.log1p(jnp.exp(-jnp.abs(z))))
        c = jnp.dot(tri, log_f, precision=lax.Precision.HIGHEST,
                    preferred_element_type=F32) + carry
        o_ref[pl.ds(start, chunk), :] = c
        return c[chunk - 1:chunk, :]

    lax.fori_loop(0, seq // chunk, body, jnp.zeros((1, LANES), F32))


def _fox_prep(flog, bias, chunk=256):
    b, s, _ = flog.shape
    return pl.pallas_call(
        functools.partial(_fox_prep_kernel, chunk=chunk),
        grid=(b,),
        in_specs=[pl.BlockSpec((None, s, LANES), lambda i: (i, 0, 0)),
                  pl.BlockSpec((1, LANES), lambda i: (0, 0))],
        out_specs=pl.BlockSpec((None, s, LANES), lambda i: (i, 0, 0)),
        out_shape=jax.ShapeDtypeStruct((b, s, LANES), F32),
        compiler_params=_params(("parallel",)),
        name="fox_prep",
    )(flog, bias)


def _qk(q, k):
    return lax.dot_general(q, k, (((1,), (1,)), ((), ())), preferred_element_type=F32)


def _softmax_step(carry, s, v):
    m, l, acc = carry
    m_new = jnp.maximum(m, jnp.max(s, axis=1, keepdims=True))
    alpha = jnp.exp(m - m_new)
    p = jnp.exp(s - m_new)
    l = alpha * l + jnp.sum(p, axis=1, keepdims=True)
    acc = alpha * acc + jnp.dot(p.astype(BF16), v, preferred_element_type=F32)
    return m_new, l, acc


def _softmax_init(t, width):
    return (jnp.full((t, 1), NEG_INF, F32), jnp.zeros((t, 1), F32), jnp.zeros((t, width), F32))


def _causal(t):
    row = lax.broadcasted_iota(jnp.int32, (t, t), 0)
    col = lax.broadcasted_iota(jnp.int32, (t, t), 1)
    return row >= col


def _fox_kernel(q_ref, k_ref, v_ref, cc_ref, cr_ref, o_ref):
    t = q_ref.shape[0]
    h = pl.program_id(1)
    qi = pl.program_id(2)
    scale = HEAD_DIM ** -0.5
    q = q_ref[...]
    lane = lax.broadcasted_iota(jnp.int32, (t, LANES), 1)
    cq = jnp.sum(jnp.where(lane == h, cc_ref[...], 0.0), axis=1, keepdims=True)

    def scores(j):
        start = pl.multiple_of(j * t, t)
        s = _qk(q, k_ref[pl.ds(start, t), :]) * scale + cq - cr_ref[:, pl.ds(start, t)]
        return s, v_ref[pl.ds(start, t), :]

    def body(j, carry):
        s, v = scores(j)
        return _softmax_step(carry, s, v)

    carry = lax.fori_loop(0, qi, body, _softmax_init(t, HEAD_DIM))
    s, v = scores(qi)
    _, l, acc = _softmax_step(carry, jnp.where(_causal(t), s, NEG_INF), v)
    o_ref[...] = (acc / l).astype(o_ref.dtype)


def _fox_attention(proj, cum_col, cum_row):
    b, s, _ = proj.shape
    t = ATT_T
    qb, kb, vb = QF_OFF // HEAD_DIM, KF_OFF // HEAD_DIM, VF_OFF // HEAD_DIM
    return pl.pallas_call(
        _fox_kernel,
        grid=(b, FOX_HEADS, s // t),
        in_specs=[pl.BlockSpec((None, t, HEAD_DIM), lambda bi, h, i: (bi, i, qb + h)),
                  pl.BlockSpec((None, s, HEAD_DIM), lambda bi, h, i: (bi, 0, kb + h)),
                  pl.BlockSpec((None, s, HEAD_DIM), lambda bi, h, i: (bi, 0, vb + h)),
                  pl.BlockSpec((None, t, LANES), lambda bi, h, i: (bi, i, 0)),
                  pl.BlockSpec((None, None, 1, s), lambda bi, h, i: (bi, h, 0, 0))],
        out_specs=pl.BlockSpec((None, t, HEAD_DIM), lambda bi, h, i: (bi, i, h)),
        out_shape=jax.ShapeDtypeStruct((b, s, FOX_WIDTH), BF16),
        compiler_params=_params(("parallel", "parallel", "arbitrary")),
        name="fox_attention",
    )(proj, proj, proj, cum_col, cum_row)


def _moba_kernel(q_ref, k_ref, v_ref, o_ref, kmean_ref):
    t = q_ref.shape[0]
    seq = k_ref.shape[0]
    nb = seq // t
    qi = pl.program_id(2)
    scale = HEAD_DIM ** -0.5

    @pl.when(qi == 0)
    def _():
        blk = lax.broadcasted_iota(jnp.int32, (LANES, seq), 0)
        pos = lax.broadcasted_iota(jnp.int32, (LANES, seq), 1)
        member = jnp.where(lax.shift_right_logical(pos, int(math.log2(t))) == blk, 1.0, 0.0)
        ksum = jnp.dot(member.astype(BF16), k_ref[...], preferred_element_type=F32)
        kmean_ref[...] = (ksum * (1.0 / t)).astype(BF16)

    q = q_ref[...]
    lane = lax.broadcasted_iota(jnp.int32, (t, LANES), 1)
    past = lane < qi
    gate = jnp.where(past, _qk(q, kmean_ref[...]), NEG_INF)

    rank = jnp.zeros((t, LANES), F32)
    for jp in range(nb):
        c = gate[:, jp:jp + 1]
        tie = jnp.where(lane > jp, 1.0, 0.0)
        rank = rank + jnp.where(c > gate, 1.0, jnp.where(c == gate, tie, 0.0))
    chosen = jnp.where(past, jnp.where(rank < float(MOBA_TOPK), 1.0, 0.0), 0.0)

    def tile(j):
        start = pl.multiple_of(j * t, t)
        return _qk(q, k_ref[pl.ds(start, t), :]) * scale, v_ref[pl.ds(start, t), :]

    s, v = tile(qi)
    carry = _softmax_step(_softmax_init(t, HEAD_DIM), jnp.where(_causal(t), s, NEG_INF), v)

    def body(j, carry):
        picked = jnp.sum(jnp.where(lane == j, chosen, 0.0), axis=1, keepdims=True)
        s, v = tile(j)
        return _softmax_step(carry, jnp.where(picked > 0.5, s, NEG_INF), v)

    _, l, acc = lax.fori_loop(0, qi, body, carry)
    o_ref[...] = (acc / l).astype(o_ref.dtype)


def _moba_attention(proj):
    b, s, _ = proj.shape
    t = MOBA_BLOCK
    assert s % t == 0 and s // t <= LANES and s // t > MOBA_TOPK
    qb, kb, vb = QM_OFF // HEAD_DIM, KM_OFF // HEAD_DIM, VM_OFF // HEAD_DIM
    return pl.pallas_call(
        _moba_kernel,
        grid=(b, MOBA_HEADS, s // t),
        in_specs=[pl.BlockSpec((None, t, HEAD_DIM), lambda bi, h, i: (bi, i, qb + h)),
                  pl.BlockSpec((None, s, HEAD_DIM), lambda bi, h, i: (bi, 0, kb + h)),
                  pl.BlockSpec((None, s, HEAD_DIM), lambda bi, h, i: (bi, 0, vb + h))],
        out_specs=pl.BlockSpec((None, t, HEAD_DIM), lambda bi, h, i: (bi, i, h)),
        out_shape=jax.ShapeDtypeStruct((b, s, MOBA_WIDTH), BF16),
        scratch_shapes=[pltpu.VMEM((LANES, HEAD_DIM), BF16)],
        compiler_params=_params(("arbitrary", "arbitrary", "arbitrary")),
        name="moba_attention",
    )(proj, proj, proj)


def _diff_kernel(q_ref, k_ref, v_ref, lq1_ref, lk1_ref, lq2_ref, lk2_ref, g_ref, o_ref, *, lam_init):
    t = q_ref.shape[0]
    qi = pl.program_id(2)
    scale = HEAD_DIM ** -0.5
    lam = (jnp.exp(jnp.sum(lq1_ref[...] * lk1_ref[...], axis=1, keepdims=True))
           - jnp.exp(jnp.sum(lq2_ref[...] * lk2_ref[...], axis=1, keepdims=True))
           + lam_init)
    q1 = q_ref[:, :HEAD_DIM]
    q2 = q_ref[:, HEAD_DIM:]

    def step(j, carry, mask):
        c1, c2 = carry
        start = pl.multiple_of(j * t, t)
        k = k_ref[pl.ds(start, t), :]
        v = v_ref[pl.ds(start, t), :]
        s1 = _qk(q1, k[:, :HEAD_DIM]) * scale
        s2 = _qk(q2, k[:, HEAD_DIM:]) * scale
        if mask is not None:
            s1 = jnp.where(mask, s1, NEG_INF)
            s2 = jnp.where(mask, s2, NEG_INF)
        return _softmax_step(c1, s1, v), _softmax_step(c2, s2, v)

    init = (_softmax_init(t, DIFF_V_DIM), _softmax_init(t, DIFF_V_DIM))
    carry = lax.fori_loop(0, qi, lambda j, c: step(j, c, None), init)
    (_, l1, a1), (_, l2, a2) = step(qi, carry, _causal(t))
    o = a1 / l1 - lam * (a2 / l2)
    ms = jnp.mean(o * o, axis=-1, keepdims=True)
    o = o * lax.rsqrt(ms + NORM_EPS) * g_ref[...] * (1.0 - lam_init)
    o_ref[...] = o.astype(o_ref.dtype)


def _diff_attention(proj, lq1, lk1, lq2, lk2, g_sub, lam_init):
    b, s, _ = proj.shape
    t = ATT_T
    w = DIFF_V_DIM
    qb, kb, vb = QD_OFF // w, KD_OFF // w, VD_OFF // w
    vec = lambda width: pl.BlockSpec((1, width), lambda bi, h, i: (0, 0))
    return pl.pallas_call(
        functools.partial(_diff_kernel, lam_init=lam_init),
        grid=(b, DIFF_HEADS, s // t),
        in_specs=[pl.BlockSpec((None, t, w), lambda bi, h, i: (bi, i, qb + h)),
                  pl.BlockSpec((None, s, w), lambda bi, h, i: (bi, 0, kb + h)),
                  pl.BlockSpec((None, s, w), lambda bi, h, i: (bi, 0, vb + h)),
                  vec(HEAD_DIM), vec(HEAD_DIM), vec(HEAD_DIM), vec(HEAD_DIM), vec(w)],
        out_specs=pl.BlockSpec((None, t, w), lambda bi, h, i: (bi, i, h)),
        out_shape=jax.ShapeDtypeStruct((b, s, DIFF_WIDTH), BF16),
        compiler_params=_params(("parallel", "parallel", "arbitrary")),
        name="diff_attention",
    )(proj, proj, proj, lq1.reshape(1, -1), lk1.reshape(1, -1), lq2.reshape(1, -1),
      lk2.reshape(1, -1), g_sub.reshape(1, -1))


def _outproj_kernel(oa_ref, ob_ref, oc_ref, wa_ref, wb_ref, wc_ref, x_ref, o_ref):
    acc = jnp.dot(oa_ref[...], wa_ref[...], preferred_element_type=F32)
    acc = acc + jnp.dot(ob_ref[...], wb_ref[...], preferred_element_type=F32)
    acc = acc + jnp.dot(oc_ref[...], wc_ref[...], preferred_element_type=F32)
    o_ref[...] = x_ref[...] + acc


def _outproj(o_a, o_b, o_c, w_out, x, tm=1024, tn=512):
    m, d = x.shape
    assert MOBA_WIDTH == FOX_WIDTH and (MOBA_WIDTH + FOX_WIDTH) % DIFF_WIDTH == 0
    return pl.pallas_call(
        _outproj_kernel,
        grid=(m // tm, d // tn),
        in_specs=[pl.BlockSpec((tm, MOBA_WIDTH), lambda i, j: (i, 0)),
                  pl.BlockSpec((tm, FOX_WIDTH), lambda i, j: (i, 0)),
                  pl.BlockSpec((tm, DIFF_WIDTH), lambda i, j: (i, 0)),
                  pl.BlockSpec((MOBA_WIDTH, tn), lambda i, j: (0, j)),
                  pl.BlockSpec((FOX_WIDTH, tn), lambda i, j: (1, j)),
                  pl.BlockSpec((DIFF_WIDTH, tn),
                               lambda i, j: ((MOBA_WIDTH + FOX_WIDTH) // DIFF_WIDTH, j)),
                  pl.BlockSpec((tm, tn), lambda i, j: (i, j))],
        out_specs=pl.BlockSpec((tm, tn), lambda i, j: (i, j)),
        out_shape=jax.ShapeDtypeStruct((m, d), F32),
        compiler_params=_params(("parallel", "arbitrary")),
        name="outproj",
    )(o_a, o_b, o_c, w_out, w_out, w_out, x)


def _ffn_up_kernel(h_ref, wg_ref, wu_ref, o_ref):
    h = h_ref[...]
    g = jnp.dot(h, wg_ref[...], preferred_element_type=F32)
    u = jnp.dot(h, wu_ref[...], preferred_element_type=F32)
    o_ref[...] = (g / (1.0 + jnp.exp(-g)) * u).astype(o_ref.dtype)


def _ffn_up(h, wg, wu, tm=1024, tn=512):
    m, d = h.shape
    f = wg.shape[1]
    return pl.pallas_call(
        _ffn_up_kernel,
        grid=(m // tm, f // tn),
        in_specs=[pl.BlockSpec((tm, d), lambda i, j: (i, 0)),
                  pl.BlockSpec((d, tn), lambda i, j: (0, j)),
                  pl.BlockSpec((d, tn), lambda i, j: (0, j))],
        out_specs=pl.BlockSpec((tm, tn), lambda i, j: (i, j)),
        out_shape=jax.ShapeDtypeStruct((m, f), BF16),
        compiler_params=_params(("parallel", "arbitrary")),
        name="ffn_up",
    )(h, wg, wu)


def _ffn_down_kernel(a_ref, w_ref, x_ref, o_ref):
    @pl.when(pl.program_id(2) == 0)
    def _():
        o_ref[...] = x_ref[...]

    o_ref[...] += jnp.dot(a_ref[...], w_ref[...], preferred_element_type=F32)


def _ffn_down(a, wd, x, tm=1024, tn=1024, tk=2816):
    m, f = a.shape
    d = wd.shape[1]
    return pl.pallas_call(
        _ffn_down_kernel,
        grid=(m // tm, d // tn, f // tk),
        in_specs=[pl.BlockSpec((tm, tk), lambda i, j, k: (i, k)),
                  pl.BlockSpec((tk, tn), lambda i, j, k: (k, j)),
                  pl.BlockSpec((tm, tn), lambda i, j, k: (i, j))],
        out_specs=pl.BlockSpec((tm, tn), lambda i, j, k: (i, j)),
        out_shape=jax.ShapeDtypeStruct((m, d), F32),
        compiler_params=_params(("parallel", "parallel", "arbitrary")),
        name="ffn_down",
    )(a, wd, x)


def _rotary_tables(seq):
    inv_freq = 1.0 / (ROPE_THETA ** (jnp.arange(0, HEAD_DIM, 2, dtype=F32) / HEAD_DIM))
    ang = jnp.arange(seq, dtype=F32)[:, None] * inv_freq[None, :]
    cos, sin = jnp.cos(ang), jnp.sin(ang)
    return jnp.concatenate([cos, cos], axis=-1), jnp.concatenate([-sin, sin], axis=-1)


def _column_gains(mq, mk, fq, fk, dq, dk):
    ones = lambda width: jnp.ones((width,), F32)
    rep = lambda g, width: jnp.tile(g.astype(F32), width // HEAD_DIM)
    return jnp.concatenate([
        rep(mq, MOBA_WIDTH), rep(mk, MOBA_WIDTH), ones(MOBA_WIDTH),
        rep(fq, FOX_WIDTH), rep(fk, FOX_WIDTH), ones(FOX_WIDTH),
        rep(dq, DIFF_QK_WIDTH), rep(dk, DIFF_QK_WIDTH), ones(DIFF_WIDTH)]).reshape(1, PROJ_WIDTH)


def kernel(x, attn_norm, w_in, moba_q_norm, moba_k_norm, fox_q_norm, fox_k_norm, fox_forget_bias, diff_q_norm, diff_k_norm, diff_lambda_q1, diff_lambda_k1, diff_lambda_q2, diff_lambda_k2, diff_sub_norm, w_out, ffn_norm, w_gate, w_up, w_down):
    b, s, d = x.shape
    depth = w_in.shape[0]
    cos_full, sin_signed = _rotary_tables(s)
    x = x.reshape(b * s, d)
    for l in range(depth):
        w_l = w_in[l]
        w_main = jnp.concatenate([w_l[:, :FLOGIT_OFF], w_l[:, FLOGIT_OFF + FOX_HEADS:]], axis=1).astype(BF16)
        w_flog = jnp.pad(w_l[:, FLOGIT_OFF:FLOGIT_OFF + FOX_HEADS], ((0, 0), (0, LANES - FOX_HEADS))).astype(BF16)
        w_o = w_out[l].astype(BF16)
        w_g = jnp.pad(w_gate[l], ((0, 0), (0, FFN_PAD - FFN_DIM))).astype(BF16)
        w_u = jnp.pad(w_up[l], ((0, 0), (0, FFN_PAD - FFN_DIM))).astype(BF16)
        w_d = jnp.pad(w_down[l], ((0, FFN_PAD - FFN_DIM), (0, 0))).astype(BF16)
        gains = _column_gains(moba_q_norm[l], moba_k_norm[l], fox_q_norm[l], fox_k_norm[l],
                              diff_q_norm[l], diff_k_norm[l])
        bias = jnp.pad(fox_forget_bias[l].astype(F32), (0, LANES - FOX_HEADS)).reshape(1, LANES)

        h = _rmsnorm(x, attn_norm[l])
        proj, flog = _inproj(h, w_main, w_flog, gains, cos_full, sin_signed, s)
        proj = proj.reshape(b, s, PROJ_WIDTH)
        cum_col = _fox_prep(flog.reshape(b, s, LANES), bias)
        cum_row = jnp.transpose(cum_col[:, :, :FOX_HEADS], (0, 2, 1)).reshape(b, FOX_HEADS, 1, s)

        o_a = _moba_attention(proj)
        o_b = _fox_attention(proj, cum_col, cum_row)
        lam_init = 0.8 - 0.6 * math.exp(-0.3 * l)
        o_c = _diff_attention(proj, diff_lambda_q1[l], diff_lambda_k1[l], diff_lambda_q2[l],
                              diff_lambda_k2[l], diff_sub_norm[l], lam_init)
        x = _outproj(o_a.reshape(b * s, -1), o_b.reshape(b * s, -1), o_c.reshape(b * s, -1), w_o, x)

        h = _rmsnorm(x, ffn_norm[l])
        a = _ffn_up(h, w_g, w_u)
        x = _ffn_down(a, w_d, x)
    return x.reshape(b, s, d)
```

```python
import functools
import math

import jax
import jax.numpy as jnp
from jax import lax
from jax.experimental import pallas as pl
from jax.experimental.pallas import tpu as pltpu

D_MODEL = 4096
HEAD_DIM = 128
DIFF_V_DIM = 2 * HEAD_DIM
DIFF_HEADS = D_MODEL // (4 * DIFF_V_DIM)
MOBA_HEADS = (D_MODEL - DIFF_HEADS * DIFF_V_DIM) // (2 * HEAD_DIM)
FOX_HEADS = MOBA_HEADS
MOBA_WIDTH = MOBA_HEADS * HEAD_DIM
FOX_WIDTH = FOX_HEADS * HEAD_DIM
DIFF_QK_WIDTH = DIFF_HEADS * 2 * HEAD_DIM
DIFF_WIDTH = DIFF_HEADS * DIFF_V_DIM
FFN_DIM = -(-8 * D_MODEL // (3 * 256)) * 256
MOBA_BLOCK = 256
MOBA_TOPK = 3
ROPE_THETA = 10000.0
NORM_EPS = 1e-6
NEG_INF = -1e30
LOG2E = math.log2(math.e)

QM_OFF = 0
KM_OFF = QM_OFF + MOBA_WIDTH
VM_OFF = KM_OFF + MOBA_WIDTH
QF_OFF = VM_OFF + MOBA_WIDTH
KF_OFF = QF_OFF + FOX_WIDTH
VF_OFF = KF_OFF + FOX_WIDTH
FLOGIT_OFF = VF_OFF + FOX_WIDTH
QD_OFF = VF_OFF + FOX_WIDTH
KD_OFF = QD_OFF + DIFF_QK_WIDTH
VD_OFF = KD_OFF + DIFF_QK_WIDTH
PROJ_WIDTH = VD_OFF + DIFF_WIDTH

LANES = 128
MXU_DIM = 256
FFN_PAD = 11264
VMEM_LIMIT = 56 * 1024 * 1024

IN_TN = 512
ATT_T = 512
CAST_ROWS = 512

F32 = jnp.float32
BF16 = jnp.bfloat16


def _params(*sem):
    return pltpu.CompilerParams(dimension_semantics=sem, vmem_limit_bytes=VMEM_LIMIT)


def _cast_rows(dst_ref, src_ref, dst_row0=0):
    rows = src_ref.shape[0]
    step = math.gcd(rows, CAST_ROWS)
    for r in range(0, rows, step):
        dst_ref[dst_row0 + r:dst_row0 + r + step, :] = src_ref[r:r + step, :].astype(BF16)


def _rms(x, g):
    ms = jnp.mean(x * x, axis=-1, keepdims=True)
    return x * lax.rsqrt(ms + NORM_EPS) * g


def _rmsnorm_kernel(x_ref, g_ref, o_ref):
    o_ref[...] = _rms(x_ref[...], g_ref[...]).astype(o_ref.dtype)


def _rmsnorm(x, g, tm=256):
    m, d = x.shape
    return pl.pallas_call(
        _rmsnorm_kernel,
        grid=(m // tm,),
        in_specs=[pl.BlockSpec((tm, d), lambda i: (i, 0)),
                  pl.BlockSpec((1, d), lambda i: (0, 0))],
        out_specs=pl.BlockSpec((tm, d), lambda i: (i, 0)),
        out_shape=jax.ShapeDtypeStruct((m, d), BF16),
        compiler_params=_params("arbitrary"),
        name="rmsnorm",
    )(x, g.reshape(1, d))


def _rmsnorm_flog_kernel(x_ref, g_ref, w_ref, o_ref, flog_ref):
    h = _rms(x_ref[...], g_ref[...]).astype(BF16)
    o_ref[...] = h
    lane = lax.broadcasted_iota(jnp.int32, w_ref.shape, 1)
    wf = jnp.where(lane < FOX_HEADS, w_ref[...], 0.0).astype(BF16)
    flog_ref[...] = jnp.dot(h, wf, preferred_element_type=F32)


def _rmsnorm_flog(x, g, w_in, layer, tm=256):
    m, d = x.shape
    assert FLOGIT_OFF % LANES == 0
    return pl.pallas_call(
        _rmsnorm_flog_kernel,
        grid=(m // tm,),
        in_specs=[pl.BlockSpec((tm, d), lambda i: (i, 0)),
                  pl.BlockSpec((1, d), lambda i: (0, 0)),
                  pl.BlockSpec((None, d, LANES), lambda i: (layer, 0, FLOGIT_OFF // LANES))],
        out_specs=[pl.BlockSpec((tm, d), lambda i: (i, 0)),
                   pl.BlockSpec((tm, LANES), lambda i: (i, 0))],
        out_shape=[jax.ShapeDtypeStruct((m, d), BF16),
                   jax.ShapeDtypeStruct((m, LANES), F32)],
        compiler_params=_params("arbitrary"),
        name="rmsnorm_flog",
    )(x, g.reshape(1, d), w_in)


SHIFTED_FROM = FLOGIT_OFF // IN_TN


def _tile_kind(n):
    def in_seg(off, width):
        return (n >= off // IN_TN) & (n < (off + width) // IN_TN)
    rot = (in_seg(QM_OFF, 2 * MOBA_WIDTH) | in_seg(QD_OFF, 2 * DIFF_QK_WIDTH))
    norm = in_seg(QF_OFF, 2 * FOX_WIDTH)
    return rot, norm


def _inproj_kernel(h_ref, wa_ref, wb_ref, gain_ref, cos_ref, sin_ref, o_ref, w_scr):
    n = pl.program_id(0)
    m = pl.program_id(1)
    d = wa_ref.shape[0]

    @pl.when((m == 0) & (n < SHIFTED_FROM))
    def _():
        _cast_rows(w_scr, wa_ref)

    @pl.when((m == 0) & (n >= SHIFTED_FROM))
    def _():
        for r in range(0, d, CAST_ROWS):
            wide = jnp.concatenate([wa_ref[r:r + CAST_ROWS, :], wb_ref[r:r + CAST_ROWS, :]], axis=1)
            w_scr[r:r + CAST_ROWS, :] = wide[:, FOX_HEADS:FOX_HEADS + IN_TN].astype(BF16)

    rot, norm = _tile_kind(n)
    heads_per_sub = MXU_DIM // HEAD_DIM

    def run(kind):
        for sub in range(IN_TN // MXU_DIM):
            c0 = sub * MXU_DIM
            acc = jnp.dot(h_ref[...], w_scr[:, c0:c0 + MXU_DIM], preferred_element_type=F32)
            if kind == "plain":
                o_ref[:, c0:c0 + MXU_DIM] = acc.astype(o_ref.dtype)
                continue
            for hh in range(heads_per_sub):
                lo = hh * HEAD_DIM
                y = _rms(acc[:, lo:lo + HEAD_DIM], gain_ref[:, c0 + lo:c0 + lo + HEAD_DIM])
                if kind == "rot":
                    y = y * cos_ref[...] + pltpu.roll(y, HEAD_DIM // 2, 1) * sin_ref[...]
                o_ref[:, c0 + lo:c0 + lo + HEAD_DIM] = y.astype(o_ref.dtype)

    pl.when(rot)(lambda: run("rot"))
    pl.when(norm)(lambda: run("norm"))
    pl.when(jnp.logical_not(rot | norm))(lambda: run("plain"))


def _inproj(h, w_in, layer, gain, cos_full, sin_signed, seq, tm=1024):
    m, d = h.shape
    n_tiles = PROJ_WIDTH // IN_TN
    pos_tiles = seq // tm
    lanes_per_tile = IN_TN // LANES
    assert FLOGIT_OFF % IN_TN == 0 and FOX_HEADS < LANES
    tail = lambda n: jnp.where(n >= SHIFTED_FROM, (n + 1) * lanes_per_tile, FLOGIT_OFF // LANES)
    return pl.pallas_call(
        _inproj_kernel,
        grid=(n_tiles, m // tm),
        in_specs=[pl.BlockSpec((tm, d), lambda n, i: (i, 0)),
                  pl.BlockSpec((None, d, IN_TN), lambda n, i: (layer, 0, n)),
                  pl.BlockSpec((None, d, LANES), lambda n, i: (layer, 0, tail(n))),
                  pl.BlockSpec((1, IN_TN), lambda n, i: (0, n)),
                  pl.BlockSpec((tm, HEAD_DIM), lambda n, i: (i % pos_tiles, 0)),
                  pl.BlockSpec((tm, HEAD_DIM), lambda n, i: (i % pos_tiles, 0))],
        out_specs=pl.BlockSpec((tm, IN_TN), lambda n, i: (i, n)),
        out_shape=jax.ShapeDtypeStruct((m, PROJ_WIDTH), BF16),
        scratch_shapes=[pltpu.VMEM((d, IN_TN), BF16)],
        compiler_params=_params("arbitrary", "arbitrary"),
        name="inproj",
    )(h, w_in, w_in, gain, cos_full, sin_signed)


def _fox_prep_kernel(fl_ref, b_ref, o_ref, *, chunk):
    seq = fl_ref.shape[0]
    row = lax.broadcasted_iota(jnp.int32, (chunk, chunk), 0)
    col = lax.broadcasted_iota(jnp.int32, (chunk, chunk), 1)
    tri = jnp.where(row >= col, 1.0, 0.0).astype(F32)

    def body(i, carry):
        start = pl.multiple_of(i * chunk, chunk)
        z = fl_ref[pl.ds(start, chunk), :] + b_ref[...]
        log_f = -(jnp.maximum(-z, 0.0) + jnp.log1p(jnp.exp(-jnp.abs(z))))
        c = jnp.dot(tri, log_f, precision=lax.Precision.HIGHEST,
                    preferred_element_type=F32) + carry
        o_ref[pl.ds(start, chunk), :] = c
        return c[chunk - 1:chunk, :]

    lax.fori_loop(0, seq // chunk, body, jnp.zeros((1, LANES), F32))


def _fox_prep(flog, bias, chunk=256):
    b, s, _ = flog.shape
    return pl.pallas_call(
        functools.partial(_fox_prep_kernel, chunk=chunk),
        grid=(b,),
        in_specs=[pl.BlockSpec((None, s, LANES), lambda i: (i, 0, 0)),
                  pl.BlockSpec((1, LANES), lambda i: (0, 0))],
        out_specs=pl.BlockSpec((None, s, LANES), lambda i: (i, 0, 0)),
        out_shape=jax.ShapeDtypeStruct((b, s, LANES), F32),
        compiler_params=_params("arbitrary"),
        name="fox_prep",
    )(flog, bias)


SCORE_SCALE = HEAD_DIM ** -0.5 * LOG2E


def _qk(q, k):
    return lax.dot_general(q, k, (((1,), (1,)), ((), ())), preferred_element_type=F32)


def _softmax_step(carry, s, v):
    m, l, acc = carry
    m_new = jnp.maximum(m, jnp.max(s, axis=1, keepdims=True))
    alpha = jnp.exp2(m - m_new)
    p = jnp.exp2(s - m_new)
    l = alpha * l + jnp.sum(p, axis=1, keepdims=True)
    acc = alpha * acc + jnp.dot(p.astype(BF16), v, preferred_element_type=F32)
    return m_new, l, acc


def _softmax_init(t, width):
    return (jnp.full((t, 1), NEG_INF, F32), jnp.zeros((t, 1), F32), jnp.zeros((t, width), F32))


def _row_col(t):
    return (lax.broadcasted_iota(jnp.int32, (t, t), 0), lax.broadcasted_iota(jnp.int32, (t, t), 1))


def _fox_kernel(q_ref, k_ref, v_ref, cc_ref, cr_ref, o_ref):
    t = q_ref.shape[0]
    h = pl.program_id(1)
    qi = pl.program_id(2)
    q = q_ref[...]
    lane = lax.broadcasted_iota(jnp.int32, (t, LANES), 1)
    cq = jnp.sum(jnp.where(lane == h, cc_ref[...], 0.0), axis=1, keepdims=True) * LOG2E

    def scores(j):
        start = pl.multiple_of(j * t, t)
        ck = cr_ref[:, pl.ds(start, t)] * LOG2E
        s = _qk(q, k_ref[pl.ds(start, t), :]) * SCORE_SCALE + cq - ck
        return s, v_ref[pl.ds(start, t), :]

    def body(j, carry):
        s, v = scores(j)
        return _softmax_step(carry, s, v)

    carry = lax.fori_loop(0, qi, body, _softmax_init(t, HEAD_DIM))
    s, v = scores(qi)
    row, col = _row_col(t)
    _, l, acc = _softmax_step(carry, jnp.where(col <= row, s, NEG_INF), v)
    o_ref[...] = (acc / l).astype(o_ref.dtype)


def _fox_attention(proj, cum_col, cum_row):
    b, s, _ = proj.shape
    t = ATT_T
    qb, kb, vb = QF_OFF // HEAD_DIM, KF_OFF // HEAD_DIM, VF_OFF // HEAD_DIM
    return pl.pallas_call(
        _fox_kernel,
        grid=(b, FOX_HEADS, s // t),
        in_specs=[pl.BlockSpec((None, t, HEAD_DIM), lambda bi, h, i: (bi, i, qb + h)),
                  pl.BlockSpec((None, s, HEAD_DIM), lambda bi, h, i: (bi, 0, kb + h)),
                  pl.BlockSpec((None, s, HEAD_DIM), lambda bi, h, i: (bi, 0, vb + h)),
                  pl.BlockSpec((None, t, LANES), lambda bi, h, i: (bi, i, 0)),
                  pl.BlockSpec((None, None, 1, s), lambda bi, h, i: (bi, h, 0, 0))],
        out_specs=pl.BlockSpec((None, t, HEAD_DIM), lambda bi, h, i: (bi, i, h)),
        out_shape=jax.ShapeDtypeStruct((b, s, FOX_WIDTH), BF16),
        compiler_params=_params("arbitrary", "arbitrary", "arbitrary"),
        name="fox_attention",
    )(proj, proj, proj, cum_col, cum_row)


def _moba_kernel(q_ref, k_ref, v_ref, o_ref, kmean_ref):
    t = q_ref.shape[0]
    blk = MOBA_BLOCK
    seq = k_ref.shape[0]
    nb = seq // blk
    nb_pad = kmean_ref.shape[0]
    p = pl.program_id(2)

    @pl.when(p == 0)
    def _():
        bidx = lax.broadcasted_iota(jnp.int32, (nb_pad, seq), 0)
        pos = lax.broadcasted_iota(jnp.int32, (nb_pad, seq), 1)
        member = jnp.where(lax.shift_right_logical(pos, int(math.log2(blk))) == bidx, 1.0, 0.0)
        ksum = jnp.dot(member.astype(BF16), k_ref[...], preferred_element_type=F32)
        kmean_ref[...] = (ksum * (1.0 / blk)).astype(BF16)

    q = q_ref[...]

    bj = lax.broadcasted_iota(jnp.int32, (nb_pad, t), 0)
    qcol = lax.broadcasted_iota(jnp.int32, (nb_pad, t), 1)
    own = 2 * p + jnp.where(qcol >= blk, 1, 0)
    past = bj < own
    gate = jnp.where(past, _qk(kmean_ref[...], q), NEG_INF)
    rank = jnp.zeros((nb_pad, t), F32)
    for jp in range(nb):
        c = gate[jp:jp + 1, :]
        tie = jnp.where(bj > jp, 1.0, 0.0)
        rank = rank + jnp.where(c > gate, 1.0, jnp.where(c == gate, tie, 0.0))
    chosen_t = jnp.where(past, jnp.where(rank < float(MOBA_TOPK), 1.0, 0.0), 0.0)
    pad = jnp.zeros((LANES - nb_pad, t), F32)
    chosen = jnp.transpose(jnp.concatenate([chosen_t, pad], axis=0))
    lane = lax.broadcasted_iota(jnp.int32, (t, LANES), 1)

    def picked(j):
        return jnp.sum(jnp.where(lane == j, chosen, 0.0), axis=1, keepdims=True)

    def tile(j):
        start = pl.multiple_of(j * t, t)
        return _qk(q, k_ref[pl.ds(start, t), :]) * SCORE_SCALE, v_ref[pl.ds(start, t), :]

    row, col = _row_col(t)
    rcol = lax.broadcasted_iota(jnp.int32, (t, 1), 0)
    allowed_row = jnp.where(rcol < blk, 1.0, picked(2 * p))
    keep = jnp.where(col <= row, jnp.where(col >= blk, 1.0, allowed_row), 0.0)
    s, v = tile(p)
    carry = _softmax_step(_softmax_init(t, HEAD_DIM), jnp.where(keep > 0.5, s, NEG_INF), v)

    def body(j, carry):
        s, v = tile(j)
        bias_lo = (picked(2 * j) - 1.0) * (-NEG_INF)
        bias_hi = (picked(2 * j + 1) - 1.0) * (-NEG_INF)
        s = jnp.concatenate([s[:, :blk] + bias_lo, s[:, blk:] + bias_hi], axis=1)
        return _softmax_step(carry, s, v)

    _, l, acc = lax.fori_loop(0, p, body, carry)
    o_ref[...] = (acc / l).astype(o_ref.dtype)


def _moba_attention(proj):
    b, s, _ = proj.shape
    t = 2 * MOBA_BLOCK
    nb = s // MOBA_BLOCK
    nb_pad = -(-nb // 8) * 8
    assert s % t == 0 and nb_pad <= LANES and nb > MOBA_TOPK
    qb, kb, vb = QM_OFF // HEAD_DIM, KM_OFF // HEAD_DIM, VM_OFF // HEAD_DIM
    return pl.pallas_call(
        _moba_kernel,
        grid=(b, MOBA_HEADS, s // t),
        in_specs=[pl.BlockSpec((None, t, HEAD_DIM), lambda bi, h, i: (bi, i, qb + h)),
                  pl.BlockSpec((None, s, HEAD_DIM), lambda bi, h, i: (bi, 0, kb + h)),
                  pl.BlockSpec((None, s, HEAD_DIM), lambda bi, h, i: (bi, 0, vb + h))],
        out_specs=pl.BlockSpec((None, t, HEAD_DIM), lambda bi, h, i: (bi, i, h)),
        out_shape=jax.ShapeDtypeStruct((b, s, MOBA_WIDTH), BF16),
        scratch_shapes=[pltpu.VMEM((nb_pad, HEAD_DIM), BF16)],
        compiler_params=_params("arbitrary", "arbitrary", "arbitrary"),
        name="moba_attention",
    )(proj, proj, proj)


def _diff_kernel(q_ref, k_ref, v_ref, lq1_ref, lk1_ref, lq2_ref, lk2_ref, g_ref, o_ref, *, lam_init):
    t = q_ref.shape[0]
    qi = pl.program_id(2)
    lam = (jnp.exp(jnp.sum(lq1_ref[...] * lk1_ref[...], axis=1, keepdims=True))
           - jnp.exp(jnp.sum(lq2_ref[...] * lk2_ref[...], axis=1, keepdims=True))
           + lam_init)
    q1 = q_ref[:, :HEAD_DIM]
    q2 = q_ref[:, HEAD_DIM:]

    def step(j, carry, mask):
        c1, c2 = carry
        start = pl.multiple_of(j * t, t)
        v = v_ref[pl.ds(start, t), :]
        s1 = _qk(q1, k_ref[pl.ds(start, t), :HEAD_DIM]) * SCORE_SCALE
        s2 = _qk(q2, k_ref[pl.ds(start, t), HEAD_DIM:]) * SCORE_SCALE
        if mask is not None:
            s1 = jnp.where(mask, s1, NEG_INF)
            s2 = jnp.where(mask, s2, NEG_INF)
        return _softmax_step(c1, s1, v), _softmax_step(c2, s2, v)

    init = (_softmax_init(t, DIFF_V_DIM), _softmax_init(t, DIFF_V_DIM))
    carry = lax.fori_loop(0, qi, lambda j, c: step(j, c, None), init)
    row, col = _row_col(t)
    (_, l1, a1), (_, l2, a2) = step(qi, carry, col <= row)
    o = a1 / l1 - lam * (a2 / l2)
    o = _rms(o, g_ref[...]) * (1.0 - lam_init)
    o_ref[...] = o.astype(o_ref.dtype)


def _diff_attention(proj, lq1, lk1, lq2, lk2, g_sub, lam_init):
    b, s, _ = proj.shape
    t = ATT_T
    w = DIFF_V_DIM
    qb, kb, vb = QD_OFF // w, KD_OFF // w, VD_OFF // w
    vec = lambda width: pl.BlockSpec((1, width), lambda bi, h, i: (0, 0))
    return pl.pallas_call(
        functools.partial(_diff_kernel, lam_init=lam_init),
        grid=(b, DIFF_HEADS, s // t),
        in_specs=[pl.BlockSpec((None, t, w), lambda bi, h, i: (bi, i, qb + h)),
                  pl.BlockSpec((None, s, w), lambda bi, h, i: (bi, 0, kb + h)),
                  pl.BlockSpec((None, s, w), lambda bi, h, i: (bi, 0, vb + h)),
                  vec(HEAD_DIM), vec(HEAD_DIM), vec(HEAD_DIM), vec(HEAD_DIM), vec(w)],
        out_specs=pl.BlockSpec((None, t, w), lambda bi, h, i: (bi, i, h)),
        out_shape=jax.ShapeDtypeStruct((b, s, DIFF_WIDTH), BF16),
        compiler_params=_params("arbitrary", "arbitrary", "arbitrary"),
        name="diff_attention",
    )(proj, proj, proj, lq1.reshape(1, -1), lk1.reshape(1, -1), lq2.reshape(1, -1),
      lk2.reshape(1, -1), g_sub.reshape(1, -1))


def _outproj_kernel(oa_ref, ob_ref, oc_ref, wa_ref, wb_ref, wc_ref, x_ref, o_ref, w_scr):
    @pl.when(pl.program_id(1) == 0)
    def _():
        _cast_rows(w_scr, wa_ref, 0)
        _cast_rows(w_scr, wb_ref, MOBA_WIDTH)
        _cast_rows(w_scr, wc_ref, MOBA_WIDTH + FOX_WIDTH)

    acc = jnp.dot(oa_ref[...], w_scr[:MOBA_WIDTH, :], preferred_element_type=F32)
    acc = acc + jnp.dot(ob_ref[...], w_scr[MOBA_WIDTH:MOBA_WIDTH + FOX_WIDTH, :],
                        preferred_element_type=F32)
    acc = acc + jnp.dot(oc_ref[...], w_scr[MOBA_WIDTH + FOX_WIDTH:, :], preferred_element_type=F32)
    o_ref[...] = x_ref[...] + acc


def _outproj(o_a, o_b, o_c, w_out, layer, x, tm=1024, tn=512):
    m, d = x.shape
    assert MOBA_WIDTH == FOX_WIDTH and (MOBA_WIDTH + FOX_WIDTH) % DIFF_WIDTH == 0
    return pl.pallas_call(
        _outproj_kernel,
        grid=(d // tn, m // tm),
        in_specs=[pl.BlockSpec((tm, MOBA_WIDTH), lambda n, i: (i, 0)),
                  pl.BlockSpec((tm, FOX_WIDTH), lambda n, i: (i, 0)),
                  pl.BlockSpec((tm, DIFF_WIDTH), lambda n, i: (i, 0)),
                  pl.BlockSpec((None, MOBA_WIDTH, tn), lambda n, i: (layer, 0, n)),
                  pl.BlockSpec((None, FOX_WIDTH, tn), lambda n, i: (layer, 1, n)),
                  pl.BlockSpec((None, DIFF_WIDTH, tn),
                               lambda n, i: (layer, (MOBA_WIDTH + FOX_WIDTH) // DIFF_WIDTH, n)),
                  pl.BlockSpec((tm, tn), lambda n, i: (i, n))],
        out_specs=pl.BlockSpec((tm, tn), lambda n, i: (i, n)),
        out_shape=jax.ShapeDtypeStruct((m, d), F32),
        scratch_shapes=[pltpu.VMEM((MOBA_WIDTH + FOX_WIDTH + DIFF_WIDTH, tn), BF16)],
        compiler_params=_params("arbitrary", "arbitrary"),
        name="outproj",
    )(o_a, o_b, o_c, w_out, w_out, w_out, x)


def _ffn_up_kernel(h_ref, wg_ref, wu_ref, o_ref, wg_scr, wu_scr, *, n_valid):
    n = pl.program_id(0)

    @pl.when((pl.program_id(1) == 0) & (n < n_valid))
    def _():
        _cast_rows(wg_scr, wg_ref)
        _cast_rows(wu_scr, wu_ref)

    @pl.when(n < n_valid)
    def _():
        h = h_ref[...]
        g = jnp.dot(h, wg_scr[...], preferred_element_type=F32)
        u = jnp.dot(h, wu_scr[...], preferred_element_type=F32)
        o_ref[...] = (g / (1.0 + jnp.exp(-g)) * u).astype(o_ref.dtype)

    @pl.when(n >= n_valid)
    def _():
        o_ref[...] = jnp.zeros(o_ref.shape, o_ref.dtype)


def _ffn_up(h, w_gate, w_up, layer, tm=1024, tn=256):
    m, d = h.shape
    n_valid = FFN_DIM // tn
    assert FFN_DIM % tn == 0 and FFN_PAD % tn == 0
    wspec = pl.BlockSpec((None, d, tn), lambda n, i: (layer, 0, jnp.minimum(n, n_valid - 1)))
    return pl.pallas_call(
        functools.partial(_ffn_up_kernel, n_valid=n_valid),
        grid=(FFN_PAD // tn, m // tm),
        in_specs=[pl.BlockSpec((tm, d), lambda n, i: (i, 0)), wspec, wspec],
        out_specs=pl.BlockSpec((tm, tn), lambda n, i: (i, n)),
        out_shape=jax.ShapeDtypeStruct((m, FFN_PAD), BF16),
        scratch_shapes=[pltpu.VMEM((d, tn), BF16), pltpu.VMEM((d, tn), BF16)],
        compiler_params=_params("arbitrary", "arbitrary"),
        name="ffn_up",
    )(h, w_gate, w_up)


def _cast_pad_kernel(w_ref, o_ref, *, n_valid):
    @pl.when(pl.program_id(0) < n_valid)
    def _():
        o_ref[...] = w_ref[...].astype(o_ref.dtype)

    @pl.when(pl.program_id(0) >= n_valid)
    def _():
        o_ref[...] = jnp.zeros(o_ref.shape, o_ref.dtype)


def _cast_pad_rows(w, layer, rows_out, tr=256):
    _, rows, cols = w.shape
    n_valid = rows // tr
    assert rows % tr == 0 and rows_out % tr == 0
    return pl.pallas_call(
        functools.partial(_cast_pad_kernel, n_valid=n_valid),
        grid=(rows_out // tr,),
        in_specs=[pl.BlockSpec((None, tr, cols), lambda i: (layer, jnp.minimum(i, n_valid - 1), 0))],
        out_specs=pl.BlockSpec((tr, cols), lambda i: (i, 0)),
        out_shape=jax.ShapeDtypeStruct((rows_out, cols), BF16),
        compiler_params=_params("arbitrary"),
        name="cast_pad_rows",
    )(w)


def _ffn_down_kernel(a_ref, w_ref, x_ref, o_ref):
    @pl.when(pl.program_id(2) == 0)
    def _():
        o_ref[...] = x_ref[...]

    o_ref[...] += jnp.dot(a_ref[...], w_ref[...], preferred_element_type=F32)


def _ffn_down(a, wd, x, tm=1024, tn=1024, tk=2816):
    m, f = a.shape
    d = wd.shape[1]
    return pl.pallas_call(
        _ffn_down_kernel,
        grid=(m // tm, d // tn, f // tk),
        in_specs=[pl.BlockSpec((tm, tk), lambda i, j, k: (i, k)),
                  pl.BlockSpec((tk, tn), lambda i, j, k: (k, j)),
                  pl.BlockSpec((tm, tn), lambda i, j, k: (i, j))],
        out_specs=pl.BlockSpec((tm, tn), lambda i, j, k: (i, j)),
        out_shape=jax.ShapeDtypeStruct((m, d), F32),
        compiler_params=_params("arbitrary", "arbitrary", "arbitrary"),
        name="ffn_down",
    )(a, wd, x)


def _rotary_tables(seq):
    inv_freq = 1.0 / (ROPE_THETA ** (jnp.arange(0, HEAD_DIM, 2, dtype=F32) / HEAD_DIM))
    ang = jnp.arange(seq, dtype=F32)[:, None] * inv_freq[None, :]
    cos, sin = jnp.cos(ang), jnp.sin(ang)
    return jnp.concatenate([cos, cos], axis=-1), jnp.concatenate([-sin, sin], axis=-1)


def _column_gains(mq, mk, fq, fk, dq, dk):
    ones = lambda width: jnp.ones((width,), F32)
    rep = lambda g, width: jnp.tile(g.astype(F32), width // HEAD_DIM)
    return jnp.concatenate([
        rep(mq, MOBA_WIDTH), rep(mk, MOBA_WIDTH), ones(MOBA_WIDTH),
        rep(fq, FOX_WIDTH), rep(fk, FOX_WIDTH), ones(FOX_WIDTH),
        rep(dq, DIFF_QK_WIDTH), rep(dk, DIFF_QK_WIDTH), ones(DIFF_WIDTH)]).reshape(1, PROJ_WIDTH)


def kernel(x, attn_norm, w_in, moba_q_norm, moba_k_norm, fox_q_norm, fox_k_norm, fox_forget_bias, diff_q_norm, diff_k_norm, diff_lambda_q1, diff_lambda_k1, diff_lambda_q2, diff_lambda_k2, diff_sub_norm, w_out, ffn_norm, w_gate, w_up, w_down):
    b, s, d = x.shape
    depth = w_in.shape[0]
    cos_full, sin_signed = _rotary_tables(s)
    x = x.reshape(b * s, d)
    for l in range(depth):
        gains = _column_gains(moba_q_norm[l], moba_k_norm[l], fox_q_norm[l], fox_k_norm[l],
                              diff_q_norm[l], diff_k_norm[l])
        bias = jnp.pad(fox_forget_bias[l].astype(F32), (0, LANES - FOX_HEADS)).reshape(1, LANES)

        h, flog = _rmsnorm_flog(x, attn_norm[l], w_in, l)
        proj = _inproj(h, w_in, l, gains, cos_full, sin_signed, s).reshape(b, s, PROJ_WIDTH)
        cum_col = _fox_prep(flog.reshape(b, s, LANES), bias)
        cum_row = jnp.transpose(cum_col[:, :, :FOX_HEADS], (0, 2, 1)).reshape(b, FOX_HEADS, 1, s)

        o_a = _moba_attention(proj)
        o_b = _fox_attention(proj, cum_col, cum_row)
        lam_init = 0.8 - 0.6 * math.exp(-0.3 * l)
        o_c = _diff_attention(proj, diff_lambda_q1[l], diff_lambda_k1[l], diff_lambda_q2[l],
                              diff_lambda_k2[l], diff_sub_norm[l], lam_init)
        x = _outproj(o_a.reshape(b * s, -1), o_b.reshape(b * s, -1), o_c.reshape(b * s, -1),
                     w_out, l, x)

        h = _rmsnorm(x, ffn_norm[l])
        a = _ffn_up(h, w_gate, w_up, l)
        x = _ffn_down(a, _cast_pad_rows(w_down, l, FFN_PAD), x)
    return x.reshape(b, s, d)
```

```python
import functools
import math

import jax
import jax.numpy as jnp
from jax import lax
from jax.experimental import pallas as pl
from jax.experimental.pallas import tpu as pltpu

D_MODEL = 4096
HEAD_DIM = 128
DIFF_V_DIM = 2 * HEAD_DIM
DIFF_HEADS = D_MODEL // (4 * DIFF_V_DIM)
MOBA_HEADS = (D_MODEL - DIFF_HEADS * DIFF_V_DIM) // (2 * HEAD_DIM)
FOX_HEADS = MOBA_HEADS
MOBA_WIDTH = MOBA_HEADS * HEAD_DIM
FOX_WIDTH = FOX_HEADS * HEAD_DIM
DIFF_QK_WIDTH = DIFF_HEADS * 2 * HEAD_DIM
DIFF_WIDTH = DIFF_HEADS * DIFF_V_DIM
FFN_DIM = -(-8 * D_MODEL // (3 * 256)) * 256
MOBA_BLOCK = 256
MOBA_TOPK = 3
ROPE_THETA = 10000.0
NORM_EPS = 1e-6
NEG_INF = -1e30
LOG2E = math.log2(math.e)

QM_OFF = 0
KM_OFF = QM_OFF + MOBA_WIDTH
VM_OFF = KM_OFF + MOBA_WIDTH
QF_OFF = VM_OFF + MOBA_WIDTH
KF_OFF = QF_OFF + FOX_WIDTH
VF_OFF = KF_OFF + FOX_WIDTH
FLOGIT_OFF = VF_OFF + FOX_WIDTH
QD_OFF = VF_OFF + FOX_WIDTH
KD_OFF = QD_OFF + DIFF_QK_WIDTH
VD_OFF = KD_OFF + DIFF_QK_WIDTH
PROJ_WIDTH = VD_OFF + DIFF_WIDTH

LANES = 128
MXU_DIM = 256
FFN_PAD = 11264
VMEM_LIMIT = 56 * 1024 * 1024

IN_TN = 512
ATT_T = 512
CAST_ROWS = 512

F32 = jnp.float32
BF16 = jnp.bfloat16


def _params(*sem):
    return pltpu.CompilerParams(dimension_semantics=sem, vmem_limit_bytes=VMEM_LIMIT)


def _cast_rows(dst_ref, src_ref, dst_row0=0):
    rows = src_ref.shape[0]
    step = math.gcd(rows, CAST_ROWS)
    for r in range(0, rows, step):
        dst_ref[dst_row0 + r:dst_row0 + r + step, :] = src_ref[r:r + step, :].astype(BF16)


def _rms(x, g):
    ms = jnp.mean(x * x, axis=-1, keepdims=True)
    return x * lax.rsqrt(ms + NORM_EPS) * g


def _rmsnorm_kernel(x_ref, g_ref, o_ref):
    o_ref[...] = _rms(x_ref[...], g_ref[...]).astype(o_ref.dtype)


def _rmsnorm(x, g, tm=256):
    m, d = x.shape
    return pl.pallas_call(
        _rmsnorm_kernel,
        grid=(m // tm,),
        in_specs=[pl.BlockSpec((tm, d), lambda i: (i, 0)),
                  pl.BlockSpec((1, d), lambda i: (0, 0))],
        out_specs=pl.BlockSpec((tm, d), lambda i: (i, 0)),
        out_shape=jax.ShapeDtypeStruct((m, d), BF16),
        compiler_params=_params("arbitrary"),
        name="rmsnorm",
    )(x, g.reshape(1, d))


def _rmsnorm_flog_kernel(x_ref, g_ref, w_ref, o_ref, flog_ref):
    h = _rms(x_ref[...], g_ref[...]).astype(BF16)
    o_ref[...] = h
    flog_ref[...] = _nt_dot(h, w_ref[...])


def _rmsnorm_flog(x, g, wt, tm=256):
    m, d = x.shape
    return pl.pallas_call(
        _rmsnorm_flog_kernel,
        grid=(m // tm,),
        in_specs=[pl.BlockSpec((tm, d), lambda i: (i, 0)),
                  pl.BlockSpec((1, d), lambda i: (0, 0)),
                  pl.BlockSpec((LANES, d), lambda i: (PROJ_WIDTH // LANES, 0))],
        out_specs=[pl.BlockSpec((tm, d), lambda i: (i, 0)),
                   pl.BlockSpec((tm, LANES), lambda i: (i, 0))],
        out_shape=[jax.ShapeDtypeStruct((m, d), BF16),
                   jax.ShapeDtypeStruct((m, LANES), F32)],
        compiler_params=_params("arbitrary"),
        name="rmsnorm_flog",
    )(x, g.reshape(1, d), wt)


WT_TR = 256
WT_ROWS = PROJ_WIDTH + WT_TR


def _prep_w_in_kernel(w_ref, *o_refs):
    i = pl.program_id(0)
    row = lax.broadcasted_iota(jnp.int32, (WT_TR, 1), 0)
    keep = jnp.where(i < PROJ_WIDTH // WT_TR, WT_TR, FOX_HEADS)
    for layer, o_ref in enumerate(o_refs):
        o_ref[...] = jnp.where(row < keep, w_ref[:, layer, :], 0.0).astype(o_ref.dtype)


def _prep_w_in(w_in):
    depth, d, width = w_in.shape
    wt = jnp.transpose(w_in, (2, 0, 1))
    n_main = PROJ_WIDTH // WT_TR
    assert FLOGIT_OFF % WT_TR == 0 and width == PROJ_WIDTH + FOX_HEADS

    def src_row(i):
        shifted = jnp.where(i >= FLOGIT_OFF // WT_TR, i * WT_TR + FOX_HEADS, i * WT_TR)
        return jnp.where(i >= n_main, FLOGIT_OFF, shifted)

    return pl.pallas_call(
        _prep_w_in_kernel,
        grid=(n_main + 1,),
        in_specs=[pl.BlockSpec((pl.Element(WT_TR), pl.Element(depth), pl.Element(d)),
                               lambda i: (src_row(i), 0, 0))],
        out_specs=[pl.BlockSpec((WT_TR, d), lambda i: (i, 0))] * depth,
        out_shape=[jax.ShapeDtypeStruct((WT_ROWS, d), BF16)] * depth,
        compiler_params=_params("arbitrary"),
        name="prep_w_in",
    )(wt)


def _nt_dot(a, b):
    return lax.dot_general(a, b, (((1,), (1,)), ((), ())), preferred_element_type=F32)


def _tile_kind(n):
    def in_seg(off, width):
        return (n >= off // IN_TN) & (n < (off + width) // IN_TN)
    rot = (in_seg(QM_OFF, 2 * MOBA_WIDTH) | in_seg(QD_OFF, 2 * DIFF_QK_WIDTH))
    norm = in_seg(QF_OFF, 2 * FOX_WIDTH)
    return rot, norm


def _inproj_kernel(h_ref, w_ref, gain_ref, cos_ref, sin_ref, o_ref, y_scr):
    rot, norm = _tile_kind(pl.program_id(1))
    heads = [slice(lo, lo + HEAD_DIM) for lo in range(0, IN_TN, HEAD_DIM)]

    @pl.when(jnp.logical_not(rot | norm))
    def _():
        o_ref[...] = _nt_dot(h_ref[...], w_ref[...]).astype(o_ref.dtype)

    @pl.when(norm)
    def _():
        acc = _nt_dot(h_ref[...], w_ref[...])
        for cols in heads:
            o_ref[:, cols] = _rms(acc[:, cols], gain_ref[:, cols]).astype(o_ref.dtype)

    @pl.when(rot)
    def _():
        acc = _nt_dot(h_ref[...], w_ref[...])
        for cols in heads:
            y_scr[:, cols] = _rms(acc[:, cols], gain_ref[:, cols])

    @pl.when(rot)
    def _():
        for cols in heads:
            y = y_scr[:, cols]
            y = y * cos_ref[...] + pltpu.roll(y, HEAD_DIM // 2, 1) * sin_ref[...]
            o_ref[:, cols] = y.astype(o_ref.dtype)


def _inproj(h, wt, gain, cos_full, sin_signed, seq, tm=1024):
    m, d = h.shape
    n_tiles = PROJ_WIDTH // IN_TN
    pos_tiles = seq // tm
    return pl.pallas_call(
        _inproj_kernel,
        grid=(m // tm, n_tiles),
        in_specs=[pl.BlockSpec((tm, d), lambda i, n: (i, 0)),
                  pl.BlockSpec((IN_TN, d), lambda i, n: (n, 0)),
                  pl.BlockSpec((1, IN_TN), lambda i, n: (0, n)),
                  pl.BlockSpec((tm, HEAD_DIM), lambda i, n: (i % pos_tiles, 0)),
                  pl.BlockSpec((tm, HEAD_DIM), lambda i, n: (i % pos_tiles, 0))],
        out_specs=pl.BlockSpec((tm, IN_TN), lambda i, n: (i, n)),
        out_shape=jax.ShapeDtypeStruct((m, PROJ_WIDTH), BF16),
        scratch_shapes=[pltpu.VMEM((tm, IN_TN), F32)],
        compiler_params=_params("arbitrary", "arbitrary"),
        name="inproj",
    )(h, wt, gain, cos_full, sin_signed)


def _fox_prep_kernel(fl_ref, b_ref, o_ref, *, chunk):
    seq = fl_ref.shape[0]
    row = lax.broadcasted_iota(jnp.int32, (chunk, chunk), 0)
    col = lax.broadcasted_iota(jnp.int32, (chunk, chunk), 1)
    tri = jnp.where(row >= col, 1.0, 0.0).astype(F32)

    def body(i, carry):
        start = pl.multiple_of(i * chunk, chunk)
        z = fl_ref[pl.ds(start, chunk), :] + b_ref[...]
        log_f = -(jnp.maximum(-z, 0.0) + jnp.log1p(jnp.exp(-jnp.abs(z))))
        c = jnp.dot(tri, log_f, precision=lax.Precision.HIGHEST,
                    preferred_element_type=F32) + carry
        o_ref[pl.ds(start, chunk), :] = c
        return c[chunk - 1:chunk, :]

    lax.fori_loop(0, seq // chunk, body, jnp.zeros((1, LANES), F32))


def _fox_prep(flog, bias, chunk=256):
    b, s, _ = flog.shape
    return pl.pallas_call(
        functools.partial(_fox_prep_kernel, chunk=chunk),
        grid=(b,),
        in_specs=[pl.BlockSpec((None, s, LANES), lambda i: (i, 0, 0)),
                  pl.BlockSpec((1, LANES), lambda i: (0, 0))],
        out_specs=pl.BlockSpec((None, s, LANES), lambda i: (i, 0, 0)),
        out_shape=jax.ShapeDtypeStruct((b, s, LANES), F32),
        compiler_params=_params("arbitrary"),
        name="fox_prep",
    )(flog, bias)


SCORE_SCALE = HEAD_DIM ** -0.5 * LOG2E


def _qk(q, k):
    return lax.dot_general(q, k, (((1,), (1,)), ((), ())), preferred_element_type=F32)


def _softmax_step(carry, s, v):
    m, l, acc = carry
    m_new = jnp.maximum(m, jnp.max(s, axis=1, keepdims=True))
    alpha = jnp.exp2(m - m_new)
    p = jnp.exp2(s - m_new)
    l = alpha * l + jnp.sum(p, axis=1, keepdims=True)
    acc = alpha * acc + jnp.dot(p.astype(BF16), v, preferred_element_type=F32)
    return m_new, l, acc


def _softmax_init(t, width):
    return (jnp.full((t, 1), NEG_INF, F32), jnp.zeros((t, 1), F32), jnp.zeros((t, width), F32))


def _row_col(t):
    return (lax.broadcasted_iota(jnp.int32, (t, t), 0), lax.broadcasted_iota(jnp.int32, (t, t), 1))


def _fox_kernel(q_ref, k_ref, v_ref, cc_ref, cr_ref, o_ref):
    t = q_ref.shape[0]
    h = pl.program_id(1)
    qi = pl.program_id(2)
    q = q_ref[...]
    lane = lax.broadcasted_iota(jnp.int32, (t, LANES), 1)
    cq = jnp.sum(jnp.where(lane == h, cc_ref[...], 0.0), axis=1, keepdims=True) * LOG2E

    def scores(j):
        start = pl.multiple_of(j * t, t)
        ck = cr_ref[:, pl.ds(start, t)] * LOG2E
        s = _qk(q, k_ref[pl.ds(start, t), :]) + cq - ck
        return s, v_ref[pl.ds(start, t), :]

    def body(j, carry):
        s, v = scores(j)
        return _softmax_step(carry, s, v)

    carry = lax.fori_loop(0, qi, body, _softmax_init(t, HEAD_DIM))
    s, v = scores(qi)
    row, col = _row_col(t)
    _, l, acc = _softmax_step(carry, jnp.where(col <= row, s, NEG_INF), v)
    o_ref[...] = (acc / l).astype(o_ref.dtype)


def _fox_attention(proj, cum_col, cum_row):
    b, s, _ = proj.shape
    t = ATT_T
    qb, kb, vb = QF_OFF // HEAD_DIM, KF_OFF // HEAD_DIM, VF_OFF // HEAD_DIM
    return pl.pallas_call(
        _fox_kernel,
        grid=(b, FOX_HEADS, s // t),
        in_specs=[pl.BlockSpec((None, t, HEAD_DIM), lambda bi, h, i: (bi, i, qb + h)),
                  pl.BlockSpec((None, s, HEAD_DIM), lambda bi, h, i: (bi, 0, kb + h)),
                  pl.BlockSpec((None, s, HEAD_DIM), lambda bi, h, i: (bi, 0, vb + h)),
                  pl.BlockSpec((None, t, LANES), lambda bi, h, i: (bi, i, 0)),
                  pl.BlockSpec((None, None, 1, s), lambda bi, h, i: (bi, h, 0, 0))],
        out_specs=pl.BlockSpec((None, t, HEAD_DIM), lambda bi, h, i: (bi, i, h)),
        out_shape=jax.ShapeDtypeStruct((b, s, FOX_WIDTH), BF16),
        compiler_params=_params("arbitrary", "arbitrary", "arbitrary"),
        name="fox_attention",
    )(proj, proj, proj, cum_col, cum_row)


def _moba_kernel(q_ref, k_ref, v_ref, o_ref, kmean_ref):
    t = q_ref.shape[0]
    blk = MOBA_BLOCK
    seq = k_ref.shape[0]
    nb = seq // blk
    nb_pad = kmean_ref.shape[0]
    p = pl.program_id(2)

    @pl.when(p == 0)
    def _():
        bidx = lax.broadcasted_iota(jnp.int32, (nb_pad, seq), 0)
        pos = lax.broadcasted_iota(jnp.int32, (nb_pad, seq), 1)
        member = jnp.where(lax.shift_right_logical(pos, int(math.log2(blk))) == bidx, 1.0, 0.0)
        ksum = jnp.dot(member.astype(BF16), k_ref[...], preferred_element_type=F32)
        kmean_ref[...] = (ksum * (1.0 / blk)).astype(BF16)

    q = q_ref[...]

    bj = lax.broadcasted_iota(jnp.int32, (nb_pad, t), 0)
    qcol = lax.broadcasted_iota(jnp.int32, (nb_pad, t), 1)
    own = 2 * p + jnp.where(qcol >= blk, 1, 0)
    past = bj < own
    gate = jnp.where(past, _qk(kmean_ref[...], q), NEG_INF)
    rank = jnp.zeros((nb_pad, t), F32)
    for jp in range(nb):
        c = gate[jp:jp + 1, :]
        tie = jnp.where(bj > jp, 1.0, 0.0)
        rank = rank + jnp.where(c > gate, 1.0, jnp.where(c == gate, tie, 0.0))
    chosen_t = jnp.where(past, jnp.where(rank < float(MOBA_TOPK), 1.0, 0.0), 0.0)
    pad = jnp.zeros((LANES - nb_pad, t), F32)
    chosen = jnp.transpose(jnp.concatenate([chosen_t, pad], axis=0))
    lane = lax.broadcasted_iota(jnp.int32, (t, LANES), 1)

    def picked(j):
        return jnp.sum(jnp.where(lane == j, chosen, 0.0), axis=1, keepdims=True)

    def tile(j):
        start = pl.multiple_of(j * t, t)
        return _qk(q, k_ref[pl.ds(start, t), :]), v_ref[pl.ds(start, t), :]

    row, col = _row_col(t)
    rcol = lax.broadcasted_iota(jnp.int32, (t, 1), 0)
    allowed_row = jnp.where(rcol < blk, 1.0, picked(2 * p))
    keep = jnp.where(col <= row, jnp.where(col >= blk, 1.0, allowed_row), 0.0)
    s, v = tile(p)
    carry = _softmax_step(_softmax_init(t, HEAD_DIM), jnp.where(keep > 0.5, s, NEG_INF), v)

    def body(j, carry):
        s, v = tile(j)
        bias_lo = (picked(2 * j) - 1.0) * (-NEG_INF)
        bias_hi = (picked(2 * j + 1) - 1.0) * (-NEG_INF)
        s = jnp.concatenate([s[:, :blk] + bias_lo, s[:, blk:] + bias_hi], axis=1)
        return _softmax_step(carry, s, v)

    _, l, acc = lax.fori_loop(0, p, body, carry)
    o_ref[...] = (acc / l).astype(o_ref.dtype)


def _moba_attention(proj):
    b, s, _ = proj.shape
    t = 2 * MOBA_BLOCK
    nb = s // MOBA_BLOCK
    nb_pad = -(-nb // 8) * 8
    assert s % t == 0 and nb_pad <= LANES and nb > MOBA_TOPK
    qb, kb, vb = QM_OFF // HEAD_DIM, KM_OFF // HEAD_DIM, VM_OFF // HEAD_DIM
    return pl.pallas_call(
        _moba_kernel,
        grid=(b, MOBA_HEADS, s // t),
        in_specs=[pl.BlockSpec((None, t, HEAD_DIM), lambda bi, h, i: (bi, i, qb + h)),
                  pl.BlockSpec((None, s, HEAD_DIM), lambda bi, h, i: (bi, 0, kb + h)),
                  pl.BlockSpec((None, s, HEAD_DIM), lambda bi, h, i: (bi, 0, vb + h))],
        out_specs=pl.BlockSpec((None, t, HEAD_DIM), lambda bi, h, i: (bi, i, h)),
        out_shape=jax.ShapeDtypeStruct((b, s, MOBA_WIDTH), BF16),
        scratch_shapes=[pltpu.VMEM((nb_pad, HEAD_DIM), BF16)],
        compiler_params=_params("arbitrary", "arbitrary", "arbitrary"),
        name="moba_attention",
    )(proj, proj, proj)


def _diff_kernel(q_ref, k_ref, v_ref, lq1_ref, lk1_ref, lq2_ref, lk2_ref, g_ref, o_ref, *, lam_init):
    t = q_ref.shape[0]
    qi = pl.program_id(2)
    lam = (jnp.exp(jnp.sum(lq1_ref[...] * lk1_ref[...], axis=1, keepdims=True))
           - jnp.exp(jnp.sum(lq2_ref[...] * lk2_ref[...], axis=1, keepdims=True))
           + lam_init)
    q1 = q_ref[:, :HEAD_DIM]
    q2 = q_ref[:, HEAD_DIM:]

    def step(j, carry, mask):
        c1, c2 = carry
        start = pl.multiple_of(j * t, t)
        v = v_ref[pl.ds(start, t), :]
        s1 = _qk(q1, k_ref[pl.ds(start, t), :HEAD_DIM])
        s2 = _qk(q2, k_ref[pl.ds(start, t), HEAD_DIM:])
        if mask is not None:
            s1 = jnp.where(mask, s1, NEG_INF)
            s2 = jnp.where(mask, s2, NEG_INF)
        return _softmax_step(c1, s1, v), _softmax_step(c2, s2, v)

    init = (_softmax_init(t, DIFF_V_DIM), _softmax_init(t, DIFF_V_DIM))
    carry = lax.fori_loop(0, qi, lambda j, c: step(j, c, None), init)
    row, col = _row_col(t)
    (_, l1, a1), (_, l2, a2) = step(qi, carry, col <= row)
    o = a1 / l1 - lam * (a2 / l2)
    o = _rms(o, g_ref[...]) * (1.0 - lam_init)
    o_ref[...] = o.astype(o_ref.dtype)


def _diff_attention(proj, lq1, lk1, lq2, lk2, g_sub, lam_init):
    b, s, _ = proj.shape
    t = ATT_T
    w = DIFF_V_DIM
    qb, kb, vb = QD_OFF // w, KD_OFF // w, VD_OFF // w
    vec = lambda width: pl.BlockSpec((1, width), lambda bi, h, i: (0, 0))
    return pl.pallas_call(
        functools.partial(_diff_kernel, lam_init=lam_init),
        grid=(b, DIFF_HEADS, s // t),
        in_specs=[pl.BlockSpec((None, t, w), lambda bi, h, i: (bi, i, qb + h)),
                  pl.BlockSpec((None, s, w), lambda bi, h, i: (bi, 0, kb + h)),
                  pl.BlockSpec((None, s, w), lambda bi, h, i: (bi, 0, vb + h)),
                  vec(HEAD_DIM), vec(HEAD_DIM), vec(HEAD_DIM), vec(HEAD_DIM), vec(w)],
        out_specs=pl.BlockSpec((None, t, w), lambda bi, h, i: (bi, i, h)),
        out_shape=jax.ShapeDtypeStruct((b, s, DIFF_WIDTH), BF16),
        compiler_params=_params("arbitrary", "arbitrary", "arbitrary"),
        name="diff_attention",
    )(proj, proj, proj, lq1.reshape(1, -1), lk1.reshape(1, -1), lq2.reshape(1, -1),
      lk2.reshape(1, -1), g_sub.reshape(1, -1))


def _outproj_kernel(oa_ref, ob_ref, oc_ref, wa_ref, wb_ref, wc_ref, x_ref, o_ref, w_scr):
    @pl.when(pl.program_id(1) == 0)
    def _():
        _cast_rows(w_scr, wa_ref, 0)
        _cast_rows(w_scr, wb_ref, MOBA_WIDTH)
        _cast_rows(w_scr, wc_ref, MOBA_WIDTH + FOX_WIDTH)

    acc = jnp.dot(oa_ref[...], w_scr[:MOBA_WIDTH, :], preferred_element_type=F32)
    acc = acc + jnp.dot(ob_ref[...], w_scr[MOBA_WIDTH:MOBA_WIDTH + FOX_WIDTH, :],
                        preferred_element_type=F32)
    acc = acc + jnp.dot(oc_ref[...], w_scr[MOBA_WIDTH + FOX_WIDTH:, :], preferred_element_type=F32)
    o_ref[...] = x_ref[...] + acc


def _outproj(o_a, o_b, o_c, w_out, layer, x, tm=1024, tn=512):
    m, d = x.shape
    assert MOBA_WIDTH == FOX_WIDTH and (MOBA_WIDTH + FOX_WIDTH) % DIFF_WIDTH == 0
    return pl.pallas_call(
        _outproj_kernel,
        grid=(d // tn, m // tm),
        in_specs=[pl.BlockSpec((tm, MOBA_WIDTH), lambda n, i: (i, 0)),
                  pl.BlockSpec((tm, FOX_WIDTH), lambda n, i: (i, 0)),
                  pl.BlockSpec((tm, DIFF_WIDTH), lambda n, i: (i, 0)),
                  pl.BlockSpec((None, MOBA_WIDTH, tn), lambda n, i: (layer, 0, n)),
                  pl.BlockSpec((None, FOX_WIDTH, tn), lambda n, i: (layer, 1, n)),
                  pl.BlockSpec((None, DIFF_WIDTH, tn),
                               lambda n, i: (layer, (MOBA_WIDTH + FOX_WIDTH) // DIFF_WIDTH, n)),
                  pl.BlockSpec((tm, tn), lambda n, i: (i, n))],
        out_specs=pl.BlockSpec((tm, tn), lambda n, i: (i, n)),
        out_shape=jax.ShapeDtypeStruct((m, d), F32),
        scratch_shapes=[pltpu.VMEM((MOBA_WIDTH + FOX_WIDTH + DIFF_WIDTH, tn), BF16)],
        compiler_params=_params("arbitrary", "arbitrary"),
        name="outproj",
    )(o_a, o_b, o_c, w_out, w_out, w_out, x)


def _ffn_up_kernel(h_ref, wg_ref, wu_ref, o_ref, wg_scr, wu_scr, *, n_valid):
    n = pl.program_id(0)

    @pl.when((pl.program_id(1) == 0) & (n < n_valid))
    def _():
        _cast_rows(wg_scr, wg_ref)
        _cast_rows(wu_scr, wu_ref)

    @pl.when(n < n_valid)
    def _():
        h = h_ref[...]
        g = jnp.dot(h, wg_scr[...], preferred_element_type=F32)
        u = jnp.dot(h, wu_scr[...], preferred_element_type=F32)
        o_ref[...] = (g / (1.0 + jnp.exp(-g)) * u).astype(o_ref.dtype)

    @pl.when(n >= n_valid)
    def _():
        o_ref[...] = jnp.zeros(o_ref.shape, o_ref.dtype)


def _ffn_up(h, w_gate, w_up, layer, tm=1024, tn=256):
    m, d = h.shape
    n_valid = FFN_DIM // tn
    assert FFN_DIM % tn == 0 and FFN_PAD % tn == 0
    wspec = pl.BlockSpec((None, d, tn), lambda n, i: (layer, 0, jnp.minimum(n, n_valid - 1)))
    return pl.pallas_call(
        functools.partial(_ffn_up_kernel, n_valid=n_valid),
        grid=(FFN_PAD // tn, m // tm),
        in_specs=[pl.BlockSpec((tm, d), lambda n, i: (i, 0)), wspec, wspec],
        out_specs=pl.BlockSpec((tm, tn), lambda n, i: (i, n)),
        out_shape=jax.ShapeDtypeStruct((m, FFN_PAD), BF16),
        scratch_shapes=[pltpu.VMEM((d, tn), BF16), pltpu.VMEM((d, tn), BF16)],
        compiler_params=_params("arbitrary", "arbitrary"),
        name="ffn_up",
    )(h, w_gate, w_up)


def _cast_pad_kernel(w_ref, o_ref, *, n_valid):
    @pl.when(pl.program_id(0) < n_valid)
    def _():
        o_ref[...] = w_ref[...].astype(o_ref.dtype)

    @pl.when(pl.program_id(0) >= n_valid)
    def _():
        o_ref[...] = jnp.zeros(o_ref.shape, o_ref.dtype)


def _cast_pad_rows(w, layer, rows_out, tr=256):
    _, rows, cols = w.shape
    n_valid = rows // tr
    assert rows % tr == 0 and rows_out % tr == 0
    return pl.pallas_call(
        functools.partial(_cast_pad_kernel, n_valid=n_valid),
        grid=(rows_out // tr,),
        in_specs=[pl.BlockSpec((None, tr, cols), lambda i: (layer, jnp.minimum(i, n_valid - 1), 0))],
        out_specs=pl.BlockSpec((tr, cols), lambda i: (i, 0)),
        out_shape=jax.ShapeDtypeStruct((rows_out, cols), BF16),
        compiler_params=_params("arbitrary"),
        name="cast_pad_rows",
    )(w)


def _ffn_down_kernel(a_ref, w_ref, x_ref, o_ref):
    @pl.when(pl.program_id(2) == 0)
    def _():
        o_ref[...] = x_ref[...]

    o_ref[...] += jnp.dot(a_ref[...], w_ref[...], preferred_element_type=F32)


def _ffn_down(a, wd, x, tm=1024, tn=1024, tk=2816):
    m, f = a.shape
    d = wd.shape[1]
    return pl.pallas_call(
        _ffn_down_kernel,
        grid=(m // tm, d // tn, f // tk),
        in_specs=[pl.BlockSpec((tm, tk), lambda i, j, k: (i, k)),
                  pl.BlockSpec((tk, tn), lambda i, j, k: (k, j)),
                  pl.BlockSpec((tm, tn), lambda i, j, k: (i, j))],
        out_specs=pl.BlockSpec((tm, tn), lambda i, j, k: (i, j)),
        out_shape=jax.ShapeDtypeStruct((m, d), F32),
        compiler_params=_params("arbitrary", "arbitrary", "arbitrary"),
        name="ffn_down",
    )(a, wd, x)


def _rotary_tables(seq):
    inv_freq = 1.0 / (ROPE_THETA ** (jnp.arange(0, HEAD_DIM, 2, dtype=F32) / HEAD_DIM))
    ang = jnp.arange(seq, dtype=F32)[:, None] * inv_freq[None, :]
    cos, sin = jnp.cos(ang), jnp.sin(ang)
    return jnp.concatenate([cos, cos], axis=-1), jnp.concatenate([-sin, sin], axis=-1)


def _column_gains(mq, mk, fq, fk, dq, dk):
    ones = lambda width: jnp.ones((width,), F32)
    rep = lambda g, width: jnp.tile(g.astype(F32), width // HEAD_DIM)
    qrep = lambda g, width: rep(g, width) * SCORE_SCALE
    return jnp.concatenate([
        qrep(mq, MOBA_WIDTH), rep(mk, MOBA_WIDTH), ones(MOBA_WIDTH),
        qrep(fq, FOX_WIDTH), rep(fk, FOX_WIDTH), ones(FOX_WIDTH),
        qrep(dq, DIFF_QK_WIDTH), rep(dk, DIFF_QK_WIDTH), ones(DIFF_WIDTH)]).reshape(1, PROJ_WIDTH)


def kernel(x, attn_norm, w_in, moba_q_norm, moba_k_norm, fox_q_norm, fox_k_norm, fox_forget_bias, diff_q_norm, diff_k_norm, diff_lambda_q1, diff_lambda_k1, diff_lambda_q2, diff_lambda_k2, diff_sub_norm, w_out, ffn_norm, w_gate, w_up, w_down):
    b, s, d = x.shape
    depth = w_in.shape[0]
    cos_full, sin_signed = _rotary_tables(s)
    x = x.reshape(b * s, d)
    w_in_t = _prep_w_in(w_in)
    for l in range(depth):
        gains = _column_gains(moba_q_norm[l], moba_k_norm[l], fox_q_norm[l], fox_k_norm[l],
                              diff_q_norm[l], diff_k_norm[l])
        bias = jnp.pad(fox_forget_bias[l].astype(F32), (0, LANES - FOX_HEADS)).reshape(1, LANES)

        h, flog = _rmsnorm_flog(x, attn_norm[l], w_in_t[l])
        proj = _inproj(h, w_in_t[l], gains, cos_full, sin_signed, s).reshape(b, s, PROJ_WIDTH)
        cum_col = _fox_prep(flog.reshape(b, s, LANES), bias)
        cum_row = jnp.transpose(cum_col[:, :, :FOX_HEADS], (0, 2, 1)).reshape(b, FOX_HEADS, 1, s)

        o_a = _moba_attention(proj)
        o_b = _fox_attention(proj, cum_col, cum_row)
        lam_init = 0.8 - 0.6 * math.exp(-0.3 * l)
        o_c = _diff_attention(proj, diff_lambda_q1[l], diff_lambda_k1[l], diff_lambda_q2[l],
                              diff_lambda_k2[l], diff_sub_norm[l], lam_init)
        x = _outproj(o_a.reshape(b * s, -1), o_b.reshape(b * s, -1), o_c.reshape(b * s, -1),
                     w_out, l, x)

        h = _rmsnorm(x, ffn_norm[l])
        a = _ffn_up(h, w_gate, w_up, l)
        x = _ffn_down(a, _cast_pad_rows(w_down, l, FFN_PAD), x)
    return x.reshape(b, s, d)
```

```python
import functools
import math

import jax
import jax.numpy as jnp
from jax import lax
from jax.experimental import pallas as pl
from jax.experimental.pallas import tpu as pltpu

D_MODEL = 4096
HEAD_DIM = 128
DIFF_V_DIM = 2 * HEAD_DIM
DIFF_HEADS = D_MODEL // (4 * DIFF_V_DIM)
MOBA_HEADS = (D_MODEL - DIFF_HEADS * DIFF_V_DIM) // (2 * HEAD_DIM)
FOX_HEADS = MOBA_HEADS
MOBA_WIDTH = MOBA_HEADS * HEAD_DIM
FOX_WIDTH = FOX_HEADS * HEAD_DIM
DIFF_QK_WIDTH = DIFF_HEADS * 2 * HEAD_DIM
DIFF_WIDTH = DIFF_HEADS * DIFF_V_DIM
FFN_DIM = -(-8 * D_MODEL // (3 * 256)) * 256
MOBA_BLOCK = 256
MOBA_TOPK = 3
ROPE_THETA = 10000.0
NORM_EPS = 1e-6
NEG_INF = -1e30
LOG2E = math.log2(math.e)

QM_OFF = 0
KM_OFF = QM_OFF + MOBA_WIDTH
VM_OFF = KM_OFF + MOBA_WIDTH
QF_OFF = VM_OFF + MOBA_WIDTH
KF_OFF = QF_OFF + FOX_WIDTH
VF_OFF = KF_OFF + FOX_WIDTH
FLOGIT_OFF = VF_OFF + FOX_WIDTH
QD_OFF = VF_OFF + FOX_WIDTH
KD_OFF = QD_OFF + DIFF_QK_WIDTH
VD_OFF = KD_OFF + DIFF_QK_WIDTH
PROJ_WIDTH = VD_OFF + DIFF_WIDTH

LANES = 128
MXU_DIM = 256
FFN_PAD = 11264
VMEM_LIMIT = 56 * 1024 * 1024

IN_TN = 512
ATT_T = 512
CAST_ROWS = 512

F32 = jnp.float32
BF16 = jnp.bfloat16


def _params(*sem):
    return pltpu.CompilerParams(dimension_semantics=sem, vmem_limit_bytes=VMEM_LIMIT)


def _cast_rows(dst_ref, src_ref, dst_row0=0):
    rows = src_ref.shape[0]
    step = math.gcd(rows, CAST_ROWS)
    for r in range(0, rows, step):
        dst_ref[dst_row0 + r:dst_row0 + r + step, :] = src_ref[r:r + step, :].astype(BF16)


def _rms(x, g):
    ms = jnp.mean(x * x, axis=-1, keepdims=True)
    return x * lax.rsqrt(ms + NORM_EPS) * g


def _rmsnorm_kernel(x_ref, g_ref, o_ref):
    o_ref[...] = _rms(x_ref[...], g_ref[...]).astype(o_ref.dtype)


def _rmsnorm(x, g, tm=256):
    m, d = x.shape
    return pl.pallas_call(
        _rmsnorm_kernel,
        grid=(m // tm,),
        in_specs=[pl.BlockSpec((tm, d), lambda i: (i, 0)),
                  pl.BlockSpec((1, d), lambda i: (0, 0))],
        out_specs=pl.BlockSpec((tm, d), lambda i: (i, 0)),
        out_shape=jax.ShapeDtypeStruct((m, d), BF16),
        compiler_params=_params("arbitrary"),
        name="rmsnorm",
    )(x, g.reshape(1, d))


def _rmsnorm_flog_kernel(x_ref, g_ref, w_ref, o_ref, flog_ref):
    h = _rms(x_ref[...], g_ref[...]).astype(BF16)
    o_ref[...] = h
    flog_ref[...] = _nt_dot(h, w_ref[...])


def _rmsnorm_flog(x, g, wt, tm=256):
    m, d = x.shape
    return pl.pallas_call(
        _rmsnorm_flog_kernel,
        grid=(m // tm,),
        in_specs=[pl.BlockSpec((tm, d), lambda i: (i, 0)),
                  pl.BlockSpec((1, d), lambda i: (0, 0)),
                  pl.BlockSpec((LANES, d), lambda i: (PROJ_WIDTH // LANES, 0))],
        out_specs=[pl.BlockSpec((tm, d), lambda i: (i, 0)),
                   pl.BlockSpec((tm, LANES), lambda i: (i, 0))],
        out_shape=[jax.ShapeDtypeStruct((m, d), BF16),
                   jax.ShapeDtypeStruct((m, LANES), F32)],
        compiler_params=_params("arbitrary"),
        name="rmsnorm_flog",
    )(x, g.reshape(1, d), wt)


WT_TR = 256
WT_ROWS = PROJ_WIDTH + WT_TR


def _prep_w_in_kernel(w_ref, *o_refs):
    i = pl.program_id(0)
    row = lax.broadcasted_iota(jnp.int32, (WT_TR, 1), 0)
    keep = jnp.where(i < PROJ_WIDTH // WT_TR, WT_TR, FOX_HEADS)
    for layer, o_ref in enumerate(o_refs):
        o_ref[...] = jnp.where(row < keep, w_ref[:, layer, :], 0.0).astype(o_ref.dtype)


def _prep_w_in(w_in):
    depth, d, width = w_in.shape
    wt = jnp.transpose(w_in, (2, 0, 1))
    n_main = PROJ_WIDTH // WT_TR
    assert FLOGIT_OFF % WT_TR == 0 and width == PROJ_WIDTH + FOX_HEADS

    def src_row(i):
        shifted = jnp.where(i >= FLOGIT_OFF // WT_TR, i * WT_TR + FOX_HEADS, i * WT_TR)
        return jnp.where(i >= n_main, FLOGIT_OFF, shifted)

    return pl.pallas_call(
        _prep_w_in_kernel,
        grid=(n_main + 1,),
        in_specs=[pl.BlockSpec((pl.Element(WT_TR), pl.Element(depth), pl.Element(d)),
                               lambda i: (src_row(i), 0, 0))],
        out_specs=[pl.BlockSpec((WT_TR, d), lambda i: (i, 0))] * depth,
        out_shape=[jax.ShapeDtypeStruct((WT_ROWS, d), BF16)] * depth,
        compiler_params=_params("arbitrary"),
        name="prep_w_in",
    )(wt)


def _nt_dot(a, b):
    return lax.dot_general(a, b, (((1,), (1,)), ((), ())), preferred_element_type=F32)


def _tile_kind(n):
    def in_seg(off, width):
        return (n >= off // IN_TN) & (n < (off + width) // IN_TN)
    rot = (in_seg(QM_OFF, 2 * MOBA_WIDTH) | in_seg(QD_OFF, 2 * DIFF_QK_WIDTH))
    norm = in_seg(QF_OFF, 2 * FOX_WIDTH)
    return rot, norm


def _inproj_kernel(h_ref, w_ref, gain_ref, cos_ref, sin_ref, o_ref, y_scr):
    rot, norm = _tile_kind(pl.program_id(1))
    heads = [slice(lo, lo + HEAD_DIM) for lo in range(0, IN_TN, HEAD_DIM)]

    @pl.when(jnp.logical_not(rot | norm))
    def _():
        o_ref[...] = _nt_dot(h_ref[...], w_ref[...]).astype(o_ref.dtype)

    @pl.when(norm)
    def _():
        acc = _nt_dot(h_ref[...], w_ref[...])
        for cols in heads:
            o_ref[:, cols] = _rms(acc[:, cols], gain_ref[:, cols]).astype(o_ref.dtype)

    @pl.when(rot)
    def _():
        acc = _nt_dot(h_ref[...], w_ref[...])
        for cols in heads:
            y_scr[:, cols] = _rms(acc[:, cols], gain_ref[:, cols])

    @pl.when(rot)
    def _():
        for cols in heads:
            y = y_scr[:, cols]
            y = y * cos_ref[...] + pltpu.roll(y, HEAD_DIM // 2, 1) * sin_ref[...]
            o_ref[:, cols] = y.astype(o_ref.dtype)


def _inproj(h, wt, gain, cos_full, sin_signed, seq, tm=1024):
    m, d = h.shape
    n_tiles = PROJ_WIDTH // IN_TN
    pos_tiles = seq // tm
    return pl.pallas_call(
        _inproj_kernel,
        grid=(m // tm, n_tiles),
        in_specs=[pl.BlockSpec((tm, d), lambda i, n: (i, 0)),
                  pl.BlockSpec((IN_TN, d), lambda i, n: (n, 0)),
                  pl.BlockSpec((1, IN_TN), lambda i, n: (0, n)),
                  pl.BlockSpec((tm, HEAD_DIM), lambda i, n: (i % pos_tiles, 0)),
                  pl.BlockSpec((tm, HEAD_DIM), lambda i, n: (i % pos_tiles, 0))],
        out_specs=pl.BlockSpec((tm, IN_TN), lambda i, n: (i, n)),
        out_shape=jax.ShapeDtypeStruct((m, PROJ_WIDTH), BF16),
        scratch_shapes=[pltpu.VMEM((tm, IN_TN), F32)],
        compiler_params=_params("arbitrary", "arbitrary"),
        name="inproj",
    )(h, wt, gain, cos_full, sin_signed)


def _fox_prep_kernel(fl_ref, b_ref, o_ref, *, chunk):
    seq = fl_ref.shape[0]
    row = lax.broadcasted_iota(jnp.int32, (chunk, chunk), 0)
    col = lax.broadcasted_iota(jnp.int32, (chunk, chunk), 1)
    tri = jnp.where(row >= col, 1.0, 0.0).astype(F32)

    def body(i, carry):
        start = pl.multiple_of(i * chunk, chunk)
        z = fl_ref[pl.ds(start, chunk), :] + b_ref[...]
        log_f = -(jnp.maximum(-z, 0.0) + jnp.log1p(jnp.exp(-jnp.abs(z))))
        c = jnp.dot(tri, log_f, precision=lax.Precision.HIGHEST,
                    preferred_element_type=F32) + carry
        o_ref[pl.ds(start, chunk), :] = c
        return c[chunk - 1:chunk, :]

    lax.fori_loop(0, seq // chunk, body, jnp.zeros((1, LANES), F32))


def _fox_prep(flog, bias, chunk=256):
    b, s, _ = flog.shape
    return pl.pallas_call(
        functools.partial(_fox_prep_kernel, chunk=chunk),
        grid=(b,),
        in_specs=[pl.BlockSpec((None, s, LANES), lambda i: (i, 0, 0)),
                  pl.BlockSpec((1, LANES), lambda i: (0, 0))],
        out_specs=pl.BlockSpec((None, s, LANES), lambda i: (i, 0, 0)),
        out_shape=jax.ShapeDtypeStruct((b, s, LANES), F32),
        compiler_params=_params("arbitrary"),
        name="fox_prep",
    )(flog, bias)


SCORE_SCALE = HEAD_DIM ** -0.5 * LOG2E


def _qk(q, k):
    return lax.dot_general(q, k, (((1,), (1,)), ((), ())), preferred_element_type=F32)


def _softmax_step(carry, s, v):
    m, l, acc = carry
    m_new = jnp.maximum(m, jnp.max(s, axis=1, keepdims=True))
    alpha = jnp.exp2(m - m_new)
    p = jnp.exp2(s - m_new)
    l = alpha * l + jnp.sum(p, axis=1, keepdims=True)
    acc = alpha * acc + jnp.dot(p.astype(BF16), v, preferred_element_type=F32)
    return m_new, l, acc


def _softmax_init(t, width):
    return (jnp.full((t, 1), NEG_INF, F32), jnp.zeros((t, 1), F32), jnp.zeros((t, width), F32))


def _row_col(t):
    return (lax.broadcasted_iota(jnp.int32, (t, t), 0), lax.broadcasted_iota(jnp.int32, (t, t), 1))


def _pipelined_causal_sweep(qi, scores, consume, s_scr, init):
    def put(s):
        for slot, part in enumerate(s):
            s_scr[slot] = part

    def get():
        return tuple(s_scr[slot] for slot in range(s_scr.shape[0]))

    put(scores(0))

    def pair(i, carry):
        j = 2 * i
        s0 = get()
        s1 = scores(j + 1)
        carry = consume(carry, j, s0, False)
        put(scores(j + 2))
        return consume(carry, j + 1, s1, False)

    def tail_one(carry):
        return consume(carry, qi, get(), True)

    def tail_two(carry):
        s_diag = scores(qi)
        carry = consume(carry, qi - 1, get(), False)
        return consume(carry, qi, s_diag, True)

    carry = lax.fori_loop(0, qi // 2, pair, init)
    return lax.cond(qi % 2 == 1, tail_two, tail_one, carry)


def _fox_kernel(q_ref, k_ref, v_ref, cc_ref, cr_ref, o_ref, s_scr):
    t = q_ref.shape[0]
    h = pl.program_id(1)
    qi = pl.program_id(2)
    q = q_ref[...]
    lane = lax.broadcasted_iota(jnp.int32, (t, LANES), 1)
    cq = jnp.sum(jnp.where(lane == h, cc_ref[...], 0.0), axis=1, keepdims=True) * LOG2E

    def scores(j):
        start = pl.multiple_of(j * t, t)
        ck = cr_ref[:, pl.ds(start, t)] * LOG2E
        return (_qk(q, k_ref[pl.ds(start, t), :]) + cq - ck,)

    def consume(carry, j, s, diagonal):
        start = pl.multiple_of(j * t, t)
        s, = s
        if diagonal:
            row, col = _row_col(t)
            s = jnp.where(col <= row, s, NEG_INF)
        return _softmax_step(carry, s, v_ref[pl.ds(start, t), :])

    _, l, acc = _pipelined_causal_sweep(qi, scores, consume, s_scr, _softmax_init(t, HEAD_DIM))
    o_ref[...] = (acc / l).astype(o_ref.dtype)


def _fox_attention(proj, cum_col, cum_row):
    b, s, _ = proj.shape
    t = ATT_T
    qb, kb, vb = QF_OFF // HEAD_DIM, KF_OFF // HEAD_DIM, VF_OFF // HEAD_DIM
    return pl.pallas_call(
        _fox_kernel,
        grid=(b, FOX_HEADS, s // t),
        in_specs=[pl.BlockSpec((None, t, HEAD_DIM), lambda bi, h, i: (bi, i, qb + h)),
                  pl.BlockSpec((None, s, HEAD_DIM), lambda bi, h, i: (bi, 0, kb + h)),
                  pl.BlockSpec((None, s, HEAD_DIM), lambda bi, h, i: (bi, 0, vb + h)),
                  pl.BlockSpec((None, t, LANES), lambda bi, h, i: (bi, i, 0)),
                  pl.BlockSpec((None, None, 1, s), lambda bi, h, i: (bi, h, 0, 0))],
        out_specs=pl.BlockSpec((None, t, HEAD_DIM), lambda bi, h, i: (bi, i, h)),
        out_shape=jax.ShapeDtypeStruct((b, s, FOX_WIDTH), BF16),
        scratch_shapes=[pltpu.VMEM((1, t, t), F32)],
        compiler_params=_params("arbitrary", "arbitrary", "arbitrary"),
        name="fox_attention",
    )(proj, proj, proj, cum_col, cum_row)


def _moba_kernel(q_ref, k_ref, v_ref, o_ref, kmean_ref, s_scr):
    t = q_ref.shape[0]
    blk = MOBA_BLOCK
    seq = k_ref.shape[0]
    nb = seq // blk
    nb_pad = kmean_ref.shape[0]
    p = pl.program_id(2)

    @pl.when(p == 0)
    def _():
        bidx = lax.broadcasted_iota(jnp.int32, (nb_pad, seq), 0)
        pos = lax.broadcasted_iota(jnp.int32, (nb_pad, seq), 1)
        member = jnp.where(lax.shift_right_logical(pos, int(math.log2(blk))) == bidx, 1.0, 0.0)
        ksum = jnp.dot(member.astype(BF16), k_ref[...], preferred_element_type=F32)
        kmean_ref[...] = (ksum * (1.0 / blk)).astype(BF16)

    q = q_ref[...]

    bj = lax.broadcasted_iota(jnp.int32, (nb_pad, t), 0)
    qcol = lax.broadcasted_iota(jnp.int32, (nb_pad, t), 1)
    own = 2 * p + jnp.where(qcol >= blk, 1, 0)
    past = bj < own
    gate = jnp.where(past, _qk(kmean_ref[...], q), NEG_INF)
    rank = jnp.zeros((nb_pad, t), F32)
    for jp in range(nb):
        c = gate[jp:jp + 1, :]
        tie = jnp.where(bj > jp, 1.0, 0.0)
        rank = rank + jnp.where(c > gate, 1.0, jnp.where(c == gate, tie, 0.0))
    chosen_t = jnp.where(past, jnp.where(rank < float(MOBA_TOPK), 1.0, 0.0), 0.0)
    pad = jnp.zeros((LANES - nb_pad, t), F32)
    chosen = jnp.transpose(jnp.concatenate([chosen_t, pad], axis=0))
    lane = lax.broadcasted_iota(jnp.int32, (t, LANES), 1)

    def picked(j):
        return jnp.sum(jnp.where(lane == j, chosen, 0.0), axis=1, keepdims=True)

    def scores(j):
        start = pl.multiple_of(j * t, t)
        return (_qk(q, k_ref[pl.ds(start, t), :]),)

    def consume(carry, j, s, diagonal):
        start = pl.multiple_of(j * t, t)
        s, = s
        if diagonal:
            row, col = _row_col(t)
            rcol = lax.broadcasted_iota(jnp.int32, (t, 1), 0)
            allowed_row = jnp.where(rcol < blk, 1.0, picked(2 * j))
            keep = jnp.where(col <= row, jnp.where(col >= blk, 1.0, allowed_row), 0.0)
            s = jnp.where(keep > 0.5, s, NEG_INF)
        else:
            bias_lo = (picked(2 * j) - 1.0) * (-NEG_INF)
            bias_hi = (picked(2 * j + 1) - 1.0) * (-NEG_INF)
            s = jnp.concatenate([s[:, :blk] + bias_lo, s[:, blk:] + bias_hi], axis=1)
        return _softmax_step(carry, s, v_ref[pl.ds(start, t), :])

    _, l, acc = _pipelined_causal_sweep(p, scores, consume, s_scr, _softmax_init(t, HEAD_DIM))
    o_ref[...] = (acc / l).astype(o_ref.dtype)


def _moba_attention(proj):
    b, s, _ = proj.shape
    t = 2 * MOBA_BLOCK
    nb = s // MOBA_BLOCK
    nb_pad = -(-nb // 8) * 8
    assert s % t == 0 and nb_pad <= LANES and nb > MOBA_TOPK
    qb, kb, vb = QM_OFF // HEAD_DIM, KM_OFF // HEAD_DIM, VM_OFF // HEAD_DIM
    return pl.pallas_call(
        _moba_kernel,
        grid=(b, MOBA_HEADS, s // t),
        in_specs=[pl.BlockSpec((None, t, HEAD_DIM), lambda bi, h, i: (bi, i, qb + h)),
                  pl.BlockSpec((None, s, HEAD_DIM), lambda bi, h, i: (bi, 0, kb + h)),
                  pl.BlockSpec((None, s, HEAD_DIM), lambda bi, h, i: (bi, 0, vb + h))],
        out_specs=pl.BlockSpec((None, t, HEAD_DIM), lambda bi, h, i: (bi, i, h)),
        out_shape=jax.ShapeDtypeStruct((b, s, MOBA_WIDTH), BF16),
        scratch_shapes=[pltpu.VMEM((nb_pad, HEAD_DIM), BF16), pltpu.VMEM((1, t, t), F32)],
        compiler_params=_params("arbitrary", "arbitrary", "arbitrary"),
        name="moba_attention",
    )(proj, proj, proj)


def _diff_kernel(q_ref, k_ref, v_ref, lq1_ref, lk1_ref, lq2_ref, lk2_ref, g_ref, o_ref, s_scr, *, lam_init):
    t = q_ref.shape[0]
    qi = pl.program_id(2)
    lam = (jnp.exp(jnp.sum(lq1_ref[...] * lk1_ref[...], axis=1, keepdims=True))
           - jnp.exp(jnp.sum(lq2_ref[...] * lk2_ref[...], axis=1, keepdims=True))
           + lam_init)
    q1 = q_ref[:, :HEAD_DIM]
    q2 = q_ref[:, HEAD_DIM:]

    def scores(j):
        start = pl.multiple_of(j * t, t)
        return (_qk(q1, k_ref[pl.ds(start, t), :HEAD_DIM]), _qk(q2, k_ref[pl.ds(start, t), HEAD_DIM:]))

    def consume(carry, j, s, diagonal):
        start = pl.multiple_of(j * t, t)
        v = v_ref[pl.ds(start, t), :]
        if diagonal:
            row, col = _row_col(t)
            s = tuple(jnp.where(col <= row, part, NEG_INF) for part in s)
        return tuple(_softmax_step(c, part, v) for c, part in zip(carry, s))

    init = (_softmax_init(t, DIFF_V_DIM), _softmax_init(t, DIFF_V_DIM))
    (_, l1, a1), (_, l2, a2) = _pipelined_causal_sweep(qi, scores, consume, s_scr, init)
    o = a1 / l1 - lam * (a2 / l2)
    o = _rms(o, g_ref[...]) * (1.0 - lam_init)
    o_ref[...] = o.astype(o_ref.dtype)


def _diff_attention(proj, lq1, lk1, lq2, lk2, g_sub, lam_init):
    b, s, _ = proj.shape
    t = ATT_T
    w = DIFF_V_DIM
    qb, kb, vb = QD_OFF // w, KD_OFF // w, VD_OFF // w
    vec = lambda width: pl.BlockSpec((1, width), lambda bi, h, i: (0, 0))
    return pl.pallas_call(
        functools.partial(_diff_kernel, lam_init=lam_init),
        grid=(b, DIFF_HEADS, s // t),
        in_specs=[pl.BlockSpec((None, t, w), lambda bi, h, i: (bi, i, qb + h)),
                  pl.BlockSpec((None, s, w), lambda bi, h, i: (bi, 0, kb + h)),
                  pl.BlockSpec((None, s, w), lambda bi, h, i: (bi, 0, vb + h)),
                  vec(HEAD_DIM), vec(HEAD_DIM), vec(HEAD_DIM), vec(HEAD_DIM), vec(w)],
        out_specs=pl.BlockSpec((None, t, w), lambda bi, h, i: (bi, i, h)),
        out_shape=jax.ShapeDtypeStruct((b, s, DIFF_WIDTH), BF16),
        scratch_shapes=[pltpu.VMEM((2, t, t), F32)],
        compiler_params=_params("arbitrary", "arbitrary", "arbitrary"),
        name="diff_attention",
    )(proj, proj, proj, lq1.reshape(1, -1), lk1.reshape(1, -1), lq2.reshape(1, -1),
      lk2.reshape(1, -1), g_sub.reshape(1, -1))


def _outproj_kernel(oa_ref, ob_ref, oc_ref, wa_ref, wb_ref, wc_ref, x_ref, o_ref, w_scr):
    @pl.when(pl.program_id(1) == 0)
    def _():
        _cast_rows(w_scr, wa_ref, 0)
        _cast_rows(w_scr, wb_ref, MOBA_WIDTH)
        _cast_rows(w_scr, wc_ref, MOBA_WIDTH + FOX_WIDTH)

    acc = jnp.dot(oa_ref[...], w_scr[:MOBA_WIDTH, :], preferred_element_type=F32)
    acc = acc + jnp.dot(ob_ref[...], w_scr[MOBA_WIDTH:MOBA_WIDTH + FOX_WIDTH, :],
                        preferred_element_type=F32)
    acc = acc + jnp.dot(oc_ref[...], w_scr[MOBA_WIDTH + FOX_WIDTH:, :], preferred_element_type=F32)
    o_ref[...] = x_ref[...] + acc


def _outproj(o_a, o_b, o_c, w_out, layer, x, tm=1024, tn=512):
    m, d = x.shape
    assert MOBA_WIDTH == FOX_WIDTH and (MOBA_WIDTH + FOX_WIDTH) % DIFF_WIDTH == 0
    return pl.pallas_call(
        _outproj_kernel,
        grid=(d // tn, m // tm),
        in_specs=[pl.BlockSpec((tm, MOBA_WIDTH), lambda n, i: (i, 0)),
                  pl.BlockSpec((tm, FOX_WIDTH), lambda n, i: (i, 0)),
                  pl.BlockSpec((tm, DIFF_WIDTH), lambda n, i: (i, 0)),
                  pl.BlockSpec((None, MOBA_WIDTH, tn), lambda n, i: (layer, 0, n)),
                  pl.BlockSpec((None, FOX_WIDTH, tn), lambda n, i: (layer, 1, n)),
                  pl.BlockSpec((None, DIFF_WIDTH, tn),
                               lambda n, i: (layer, (MOBA_WIDTH + FOX_WIDTH) // DIFF_WIDTH, n)),
                  pl.BlockSpec((tm, tn), lambda n, i: (i, n))],
        out_specs=pl.BlockSpec((tm, tn), lambda n, i: (i, n)),
        out_shape=jax.ShapeDtypeStruct((m, d), F32),
        scratch_shapes=[pltpu.VMEM((MOBA_WIDTH + FOX_WIDTH + DIFF_WIDTH, tn), BF16)],
        compiler_params=_params("arbitrary", "arbitrary"),
        name="outproj",
    )(o_a, o_b, o_c, w_out, w_out, w_out, x)


def _ffn_up_kernel(h_ref, wg_ref, wu_ref, o_ref, wg_scr, wu_scr, *, n_valid):
    n = pl.program_id(0)

    @pl.when((pl.program_id(1) == 0) & (n < n_valid))
    def _():
        _cast_rows(wg_scr, wg_ref)
        _cast_rows(wu_scr, wu_ref)

    @pl.when(n < n_valid)
    def _():
        h = h_ref[...]
        g = jnp.dot(h, wg_scr[...], preferred_element_type=F32)
        u = jnp.dot(h, wu_scr[...], preferred_element_type=F32)
        o_ref[...] = (g / (1.0 + jnp.exp(-g)) * u).astype(o_ref.dtype)

    @pl.when(n >= n_valid)
    def _():
        o_ref[...] = jnp.zeros(o_ref.shape, o_ref.dtype)


def _ffn_up(h, w_gate, w_up, layer, tm=1024, tn=256):
    m, d = h.shape
    n_valid = FFN_DIM // tn
    assert FFN_DIM % tn == 0 and FFN_PAD % tn == 0
    wspec = pl.BlockSpec((None, d, tn), lambda n, i: (layer, 0, jnp.minimum(n, n_valid - 1)))
    return pl.pallas_call(
        functools.partial(_ffn_up_kernel, n_valid=n_valid),
        grid=(FFN_PAD // tn, m // tm),
        in_specs=[pl.BlockSpec((tm, d), lambda n, i: (i, 0)), wspec, wspec],
        out_specs=pl.BlockSpec((tm, tn), lambda n, i: (i, n)),
        out_shape=jax.ShapeDtypeStruct((m, FFN_PAD), BF16),
        scratch_shapes=[pltpu.VMEM((d, tn), BF16), pltpu.VMEM((d, tn), BF16)],
        compiler_params=_params("arbitrary", "arbitrary"),
        name="ffn_up",
    )(h, w_gate, w_up)


def _cast_pad_kernel(w_ref, o_ref, *, n_valid):
    @pl.when(pl.program_id(0) < n_valid)
    def _():
        o_ref[...] = w_ref[...].astype(o_ref.dtype)

    @pl.when(pl.program_id(0) >= n_valid)
    def _():
        o_ref[...] = jnp.zeros(o_ref.shape, o_ref.dtype)


def _cast_pad_rows(w, layer, rows_out, tr=256):
    _, rows, cols = w.shape
    n_valid = rows // tr
    assert rows % tr == 0 and rows_out % tr == 0
    return pl.pallas_call(
        functools.partial(_cast_pad_kernel, n_valid=n_valid),
        grid=(rows_out // tr,),
        in_specs=[pl.BlockSpec((None, tr, cols), lambda i: (layer, jnp.minimum(i, n_valid - 1), 0))],
        out_specs=pl.BlockSpec((tr, cols), lambda i: (i, 0)),
        out_shape=jax.ShapeDtypeStruct((rows_out, cols), BF16),
        compiler_params=_params("arbitrary"),
        name="cast_pad_rows",
    )(w)


def _ffn_down_kernel(a_ref, w_ref, x_ref, o_ref):
    @pl.when(pl.program_id(2) == 0)
    def _():
        o_ref[...] = x_ref[...]

    o_ref[...] += jnp.dot(a_ref[...], w_ref[...], preferred_element_type=F32)


def _ffn_down(a, wd, x, tm=1024, tn=1024, tk=2816):
    m, f = a.shape
    d = wd.shape[1]
    return pl.pallas_call(
        _ffn_down_kernel,
        grid=(m // tm, d // tn, f // tk),
        in_specs=[pl.BlockSpec((tm, tk), lambda i, j, k: (i, k)),
                  pl.BlockSpec((tk, tn), lambda i, j, k: (k, j)),
                  pl.BlockSpec((tm, tn), lambda i, j, k: (i, j))],
        out_specs=pl.BlockSpec((tm, tn), lambda i, j, k: (i, j)),
        out_shape=jax.ShapeDtypeStruct((m, d), F32),
        compiler_params=_params("arbitrary", "arbitrary", "arbitrary"),
        name="ffn_down",
    )(a, wd, x)


def _rotary_tables(seq):
    inv_freq = 1.0 / (ROPE_THETA ** (jnp.arange(0, HEAD_DIM, 2, dtype=F32) / HEAD_DIM))
    ang = jnp.arange(seq, dtype=F32)[:, None] * inv_freq[None, :]
    cos, sin = jnp.cos(ang), jnp.sin(ang)
    return jnp.concatenate([cos, cos], axis=-1), jnp.concatenate([-sin, sin], axis=-1)


def _column_gains(mq, mk, fq, fk, dq, dk):
    ones = lambda width: jnp.ones((width,), F32)
    rep = lambda g, width: jnp.tile(g.astype(F32), width // HEAD_DIM)
    qrep = lambda g, width: rep(g, width) * SCORE_SCALE
    return jnp.concatenate([
        qrep(mq, MOBA_WIDTH), rep(mk, MOBA_WIDTH), ones(MOBA_WIDTH),
        qrep(fq, FOX_WIDTH), rep(fk, FOX_WIDTH), ones(FOX_WIDTH),
        qrep(dq, DIFF_QK_WIDTH), rep(dk, DIFF_QK_WIDTH), ones(DIFF_WIDTH)]).reshape(1, PROJ_WIDTH)


def kernel(x, attn_norm, w_in, moba_q_norm, moba_k_norm, fox_q_norm, fox_k_norm, fox_forget_bias, diff_q_norm, diff_k_norm, diff_lambda_q1, diff_lambda_k1, diff_lambda_q2, diff_lambda_k2, diff_sub_norm, w_out, ffn_norm, w_gate, w_up, w_down):
    b, s, d = x.shape
    depth = w_in.shape[0]
    cos_full, sin_signed = _rotary_tables(s)
    x = x.reshape(b * s, d)
    w_in_t = _prep_w_in(w_in)
    for l in range(depth):
        gains = _column_gains(moba_q_norm[l], moba_k_norm[l], fox_q_norm[l], fox_k_norm[l],
                              diff_q_norm[l], diff_k_norm[l])
        bias = jnp.pad(fox_forget_bias[l].astype(F32), (0, LANES - FOX_HEADS)).reshape(1, LANES)

        h, flog = _rmsnorm_flog(x, attn_norm[l], w_in_t[l])
        proj = _inproj(h, w_in_t[l], gains, cos_full, sin_signed, s).reshape(b, s, PROJ_WIDTH)
        cum_col = _fox_prep(flog.reshape(b, s, LANES), bias)
        cum_row = jnp.transpose(cum_col[:, :, :FOX_HEADS], (0, 2, 1)).reshape(b, FOX_HEADS, 1, s)

        o_a = _moba_attention(proj)
        o_b = _fox_attention(proj, cum_col, cum_row)
        lam_init = 0.8 - 0.6 * math.exp(-0.3 * l)
        o_c = _diff_attention(proj, diff_lambda_q1[l], diff_lambda_k1[l], diff_lambda_q2[l],
                              diff_lambda_k2[l], diff_sub_norm[l], lam_init)
        x = _outproj(o_a.reshape(b * s, -1), o_b.reshape(b * s, -1), o_c.reshape(b * s, -1),
                     w_out, l, x)

        h = _rmsnorm(x, ffn_norm[l])
        a = _ffn_up(h, w_gate, w_up, l)
        x = _ffn_down(a, _cast_pad_rows(w_down, l, FFN_PAD), x)
    return x.reshape(b, s, d)
```

```python
import functools
import math

import jax
import jax.numpy as jnp
from jax import lax
from jax.experimental import pallas as pl
from jax.experimental.pallas import tpu as pltpu

D_MODEL = 4096
HEAD_DIM = 128
DIFF_V_DIM = 2 * HEAD_DIM
DIFF_HEADS = D_MODEL // (4 * DIFF_V_DIM)
MOBA_HEADS = (D_MODEL - DIFF_HEADS * DIFF_V_DIM) // (2 * HEAD_DIM)
FOX_HEADS = MOBA_HEADS
MOBA_WIDTH = MOBA_HEADS * HEAD_DIM
FOX_WIDTH = FOX_HEADS * HEAD_DIM
DIFF_QK_WIDTH = DIFF_HEADS * 2 * HEAD_DIM
DIFF_WIDTH = DIFF_HEADS * DIFF_V_DIM
FFN_DIM = -(-8 * D_MODEL // (3 * 256)) * 256
MOBA_BLOCK = 256
MOBA_TOPK = 3
ROPE_THETA = 10000.0
NORM_EPS = 1e-6
NEG_INF = -1e30
LOG2E = math.log2(math.e)

QM_OFF = 0
KM_OFF = QM_OFF + MOBA_WIDTH
VM_OFF = KM_OFF + MOBA_WIDTH
QF_OFF = VM_OFF + MOBA_WIDTH
KF_OFF = QF_OFF + FOX_WIDTH
VF_OFF = KF_OFF + FOX_WIDTH
FLOGIT_OFF = VF_OFF + FOX_WIDTH
QD_OFF = VF_OFF + FOX_WIDTH
KD_OFF = QD_OFF + DIFF_QK_WIDTH
VD_OFF = KD_OFF + DIFF_QK_WIDTH
PROJ_WIDTH = VD_OFF + DIFF_WIDTH

LANES = 128
MXU_DIM = 256
FFN_PAD = 11264
VMEM_LIMIT = 56 * 1024 * 1024

IN_TN = 512
ATT_T = 512
CAST_ROWS = 512

F32 = jnp.float32
BF16 = jnp.bfloat16


def _params(*sem):
    return pltpu.CompilerParams(dimension_semantics=sem, vmem_limit_bytes=VMEM_LIMIT)


def _cast_rows(dst_ref, src_ref, dst_row0=0):
    rows = src_ref.shape[0]
    step = math.gcd(rows, CAST_ROWS)
    for r in range(0, rows, step):
        dst_ref[dst_row0 + r:dst_row0 + r + step, :] = src_ref[r:r + step, :].astype(BF16)


def _rms(x, g):
    ms = jnp.mean(x * x, axis=-1, keepdims=True)
    return x * lax.rsqrt(ms + NORM_EPS) * g


def _rmsnorm_kernel(x_ref, g_ref, o_ref):
    o_ref[...] = _rms(x_ref[...], g_ref[...]).astype(o_ref.dtype)


def _rmsnorm(x, g, tm=256):
    m, d = x.shape
    return pl.pallas_call(
        _rmsnorm_kernel,
        grid=(m // tm,),
        in_specs=[pl.BlockSpec((tm, d), lambda i: (i, 0)),
                  pl.BlockSpec((1, d), lambda i: (0, 0))],
        out_specs=pl.BlockSpec((tm, d), lambda i: (i, 0)),
        out_shape=jax.ShapeDtypeStruct((m, d), BF16),
        compiler_params=_params("arbitrary"),
        name="rmsnorm",
    )(x, g.reshape(1, d))


def _rmsnorm_flog_kernel(x_ref, g_ref, w_ref, o_ref, flog_ref):
    h = _rms(x_ref[...], g_ref[...]).astype(BF16)
    o_ref[...] = h
    flog_ref[...] = _nt_dot(h, w_ref[...])


def _rmsnorm_flog(x, g, wt, tm=256):
    m, d = x.shape
    return pl.pallas_call(
        _rmsnorm_flog_kernel,
        grid=(m // tm,),
        in_specs=[pl.BlockSpec((tm, d), lambda i: (i, 0)),
                  pl.BlockSpec((1, d), lambda i: (0, 0)),
                  pl.BlockSpec((LANES, d), lambda i: (PROJ_WIDTH // LANES, 0))],
        out_specs=[pl.BlockSpec((tm, d), lambda i: (i, 0)),
                   pl.BlockSpec((tm, LANES), lambda i: (i, 0))],
        out_shape=[jax.ShapeDtypeStruct((m, d), BF16),
                   jax.ShapeDtypeStruct((m, LANES), F32)],
        compiler_params=_params("arbitrary"),
        name="rmsnorm_flog",
    )(x, g.reshape(1, d), wt)


WT_TR = 256
WT_ROWS = PROJ_WIDTH + WT_TR


def _prep_w_in_kernel(w_ref, *o_refs):
    i = pl.program_id(0)
    row = lax.broadcasted_iota(jnp.int32, (WT_TR, 1), 0)
    keep = jnp.where(i < PROJ_WIDTH // WT_TR, WT_TR, FOX_HEADS)
    for layer, o_ref in enumerate(o_refs):
        o_ref[...] = jnp.where(row < keep, w_ref[:, layer, :], 0.0).astype(o_ref.dtype)


def _prep_w_in(w_in):
    depth, d, width = w_in.shape
    wt = jnp.transpose(w_in, (2, 0, 1))
    n_main = PROJ_WIDTH // WT_TR
    assert FLOGIT_OFF % WT_TR == 0 and width == PROJ_WIDTH + FOX_HEADS

    def src_row(i):
        shifted = jnp.where(i >= FLOGIT_OFF // WT_TR, i * WT_TR + FOX_HEADS, i * WT_TR)
        return jnp.where(i >= n_main, FLOGIT_OFF, shifted)

    return pl.pallas_call(
        _prep_w_in_kernel,
        grid=(n_main + 1,),
        in_specs=[pl.BlockSpec((pl.Element(WT_TR), pl.Element(depth), pl.Element(d)),
                               lambda i: (src_row(i), 0, 0))],
        out_specs=[pl.BlockSpec((WT_TR, d), lambda i: (i, 0))] * depth,
        out_shape=[jax.ShapeDtypeStruct((WT_ROWS, d), BF16)] * depth,
        compiler_params=_params("arbitrary"),
        name="prep_w_in",
    )(wt)


def _nt_dot(a, b):
    return lax.dot_general(a, b, (((1,), (1,)), ((), ())), preferred_element_type=F32)


def _tile_kind(n):
    def in_seg(off, width):
        return (n >= off // IN_TN) & (n < (off + width) // IN_TN)
    rot = (in_seg(QM_OFF, 2 * MOBA_WIDTH) | in_seg(QD_OFF, 2 * DIFF_QK_WIDTH))
    norm = in_seg(QF_OFF, 2 * FOX_WIDTH)
    return rot, norm


IN_TILES_PER_STEP = 2


def _inproj_kernel(h_ref, w_ref, gain_ref, cos_ref, sin_ref, o_ref, y_scr):
    for sub in range(IN_TILES_PER_STEP):
        rot, norm = _tile_kind(IN_TILES_PER_STEP * pl.program_id(1) + sub)
        c0 = sub * IN_TN
        tile = slice(c0, c0 + IN_TN)
        heads = [(slice(lo, lo + HEAD_DIM), slice(c0 + lo, c0 + lo + HEAD_DIM))
                 for lo in range(0, IN_TN, HEAD_DIM)]

        @pl.when(jnp.logical_not(rot | norm))
        def _():
            o_ref[:, tile] = _nt_dot(h_ref[...], w_ref[tile, :]).astype(o_ref.dtype)

        @pl.when(norm)
        def _():
            acc = _nt_dot(h_ref[...], w_ref[tile, :])
            for loc, glob in heads:
                o_ref[:, glob] = _rms(acc[:, loc], gain_ref[:, glob]).astype(o_ref.dtype)

        @pl.when(rot)
        def _():
            acc = _nt_dot(h_ref[...], w_ref[tile, :])
            for loc, glob in heads:
                y_scr[:, loc] = _rms(acc[:, loc], gain_ref[:, glob])

        @pl.when(rot)
        def _():
            for loc, glob in heads:
                y = y_scr[:, loc]
                y = y * cos_ref[...] + pltpu.roll(y, HEAD_DIM // 2, 1) * sin_ref[...]
                o_ref[:, glob] = y.astype(o_ref.dtype)


def _inproj(h, wt, gain, cos_full, sin_signed, seq, tm=1024):
    m, d = h.shape
    tn = IN_TILES_PER_STEP * IN_TN
    assert PROJ_WIDTH % tn == 0
    pos_tiles = seq // tm
    return pl.pallas_call(
        _inproj_kernel,
        grid=(m // tm, PROJ_WIDTH // tn),
        in_specs=[pl.BlockSpec((tm, d), lambda i, n: (i, 0)),
                  pl.BlockSpec((tn, d), lambda i, n: (n, 0)),
                  pl.BlockSpec((1, tn), lambda i, n: (0, n)),
                  pl.BlockSpec((tm, HEAD_DIM), lambda i, n: (i % pos_tiles, 0)),
                  pl.BlockSpec((tm, HEAD_DIM), lambda i, n: (i % pos_tiles, 0))],
        out_specs=pl.BlockSpec((tm, tn), lambda i, n: (i, n)),
        out_shape=jax.ShapeDtypeStruct((m, PROJ_WIDTH), BF16),
        scratch_shapes=[pltpu.VMEM((tm, IN_TN), F32)],
        compiler_params=_params("arbitrary", "arbitrary"),
        name="inproj",
    )(h, wt, gain, cos_full, sin_signed)


def _fox_prep_kernel(fl_ref, b_ref, o_ref, *, chunk):
    seq = fl_ref.shape[0]
    row = lax.broadcasted_iota(jnp.int32, (chunk, chunk), 0)
    col = lax.broadcasted_iota(jnp.int32, (chunk, chunk), 1)
    tri = jnp.where(row >= col, 1.0, 0.0).astype(F32)

    def body(i, carry):
        start = pl.multiple_of(i * chunk, chunk)
        z = fl_ref[pl.ds(start, chunk), :] + b_ref[...]
        log_f = -(jnp.maximum(-z, 0.0) + jnp.log1p(jnp.exp(-jnp.abs(z))))
        c = jnp.dot(tri, log_f, precision=lax.Precision.HIGHEST,
                    preferred_element_type=F32) + carry
        o_ref[pl.ds(start, chunk), :] = c
        return c[chunk - 1:chunk, :]

    lax.fori_loop(0, seq // chunk, body, jnp.zeros((1, LANES), F32))


def _fox_prep(flog, bias, chunk=256):
    b, s, _ = flog.shape
    return pl.pallas_call(
        functools.partial(_fox_prep_kernel, chunk=chunk),
        grid=(b,),
        in_specs=[pl.BlockSpec((None, s, LANES), lambda i: (i, 0, 0)),
                  pl.BlockSpec((1, LANES), lambda i: (0, 0))],
        out_specs=pl.BlockSpec((None, s, LANES), lambda i: (i, 0, 0)),
        out_shape=jax.ShapeDtypeStruct((b, s, LANES), F32),
        compiler_params=_params("arbitrary"),
        name="fox_prep",
    )(flog, bias)


SCORE_SCALE = HEAD_DIM ** -0.5 * LOG2E


def _qk(q, k):
    return lax.dot_general(q, k, (((1,), (1,)), ((), ())), preferred_element_type=F32)


def _softmax_step(carry, s, v):
    m, l, acc = carry
    m_new = jnp.maximum(m, jnp.max(s, axis=1, keepdims=True))
    alpha = jnp.exp2(m - m_new)
    p = jnp.exp2(s - m_new)
    l = alpha * l + jnp.sum(p, axis=1, keepdims=True)
    acc = alpha * acc + jnp.dot(p.astype(BF16), v, preferred_element_type=F32)
    return m_new, l, acc


def _softmax_init(t, width):
    return (jnp.full((t, 1), NEG_INF, F32), jnp.zeros((t, 1), F32), jnp.zeros((t, width), F32))


def _row_col(t):
    return (lax.broadcasted_iota(jnp.int32, (t, t), 0), lax.broadcasted_iota(jnp.int32, (t, t), 1))


def _pipelined_causal_sweep(qi, scores, consume, s_scr, init):
    def put(s):
        for slot, part in enumerate(s):
            s_scr[slot] = part

    def get():
        return tuple(s_scr[slot] for slot in range(s_scr.shape[0]))

    put(scores(0))

    def pair(i, carry):
        j = 2 * i
        s0 = get()
        s1 = scores(j + 1)
        carry = consume(carry, j, s0, False)
        put(scores(j + 2))
        return consume(carry, j + 1, s1, False)

    def tail_one(carry):
        return consume(carry, qi, get(), True)

    def tail_two(carry):
        s_diag = scores(qi)
        carry = consume(carry, qi - 1, get(), False)
        return consume(carry, qi, s_diag, True)

    carry = lax.fori_loop(0, qi // 2, pair, init)
    return lax.cond(qi % 2 == 1, tail_two, tail_one, carry)


def _fox_kernel(q_ref, k_ref, v_ref, cc_ref, cr_ref, o_ref, s_scr):
    t = q_ref.shape[0]
    h = pl.program_id(1)
    qi = pl.program_id(2)
    q = q_ref[...]
    lane = lax.broadcasted_iota(jnp.int32, (t, LANES), 1)
    cq = jnp.sum(jnp.where(lane == h, cc_ref[...], 0.0), axis=1, keepdims=True) * LOG2E

    def scores(j):
        start = pl.multiple_of(j * t, t)
        ck = cr_ref[:, pl.ds(start, t)] * LOG2E
        return (_qk(q, k_ref[pl.ds(start, t), :]) + cq - ck,)

    def consume(carry, j, s, diagonal):
        start = pl.multiple_of(j * t, t)
        s, = s
        if diagonal:
            row, col = _row_col(t)
            s = jnp.where(col <= row, s, NEG_INF)
        return _softmax_step(carry, s, v_ref[pl.ds(start, t), :])

    _, l, acc = _pipelined_causal_sweep(qi, scores, consume, s_scr, _softmax_init(t, HEAD_DIM))
    o_ref[...] = (acc / l).astype(o_ref.dtype)


def _fox_attention(proj, cum_col, cum_row):
    b, s, _ = proj.shape
    t = ATT_T
    qb, kb, vb = QF_OFF // HEAD_DIM, KF_OFF // HEAD_DIM, VF_OFF // HEAD_DIM
    return pl.pallas_call(
        _fox_kernel,
        grid=(b, FOX_HEADS, s // t),
        in_specs=[pl.BlockSpec((None, t, HEAD_DIM), lambda bi, h, i: (bi, i, qb + h)),
                  pl.BlockSpec((None, s, HEAD_DIM), lambda bi, h, i: (bi, 0, kb + h)),
                  pl.BlockSpec((None, s, HEAD_DIM), lambda bi, h, i: (bi, 0, vb + h)),
                  pl.BlockSpec((None, t, LANES), lambda bi, h, i: (bi, i, 0)),
                  pl.BlockSpec((None, None, 1, s), lambda bi, h, i: (bi, h, 0, 0))],
        out_specs=pl.BlockSpec((None, t, HEAD_DIM), lambda bi, h, i: (bi, i, h)),
        out_shape=jax.ShapeDtypeStruct((b, s, FOX_WIDTH), BF16),
        scratch_shapes=[pltpu.VMEM((1, t, t), F32)],
        compiler_params=_params("arbitrary", "arbitrary", "arbitrary"),
        name="fox_attention",
    )(proj, proj, proj, cum_col, cum_row)


def _moba_kernel(q_ref, k_ref, v_ref, o_ref, kmean_ref, s_scr):
    t = q_ref.shape[0]
    blk = MOBA_BLOCK
    seq = k_ref.shape[0]
    nb = seq // blk
    nb_pad = kmean_ref.shape[0]
    p = pl.program_id(2)

    @pl.when(p == 0)
    def _():
        bidx = lax.broadcasted_iota(jnp.int32, (nb_pad, seq), 0)
        pos = lax.broadcasted_iota(jnp.int32, (nb_pad, seq), 1)
        member = jnp.where(lax.shift_right_logical(pos, int(math.log2(blk))) == bidx, 1.0, 0.0)
        ksum = jnp.dot(member.astype(BF16), k_ref[...], preferred_element_type=F32)
        kmean_ref[...] = (ksum * (1.0 / blk)).astype(BF16)

    q = q_ref[...]

    bj = lax.broadcasted_iota(jnp.int32, (nb_pad, t), 0)
    qcol = lax.broadcasted_iota(jnp.int32, (nb_pad, t), 1)
    own = 2 * p + jnp.where(qcol >= blk, 1, 0)
    past = bj < own
    gate = jnp.where(past, _qk(kmean_ref[...], q), NEG_INF)
    rank = jnp.zeros((nb_pad, t), F32)
    for jp in range(nb):
        c = gate[jp:jp + 1, :]
        tie = jnp.where(bj > jp, 1.0, 0.0)
        rank = rank + jnp.where(c > gate, 1.0, jnp.where(c == gate, tie, 0.0))
    chosen_t = jnp.where(past, jnp.where(rank < float(MOBA_TOPK), 1.0, 0.0), 0.0)
    pad = jnp.zeros((LANES - nb_pad, t), F32)
    chosen = jnp.transpose(jnp.concatenate([chosen_t, pad], axis=0))
    lane = lax.broadcasted_iota(jnp.int32, (t, LANES), 1)

    def picked(j):
        return jnp.sum(jnp.where(lane == j, chosen, 0.0), axis=1, keepdims=True)

    def scores(j):
        start = pl.multiple_of(j * t, t)
        return (_qk(q, k_ref[pl.ds(start, t), :]),)

    def consume(carry, j, s, diagonal):
        start = pl.multiple_of(j * t, t)
        s, = s
        if diagonal:
            row, col = _row_col(t)
            rcol = lax.broadcasted_iota(jnp.int32, (t, 1), 0)
            allowed_row = jnp.where(rcol < blk, 1.0, picked(2 * j))
            keep = jnp.where(col <= row, jnp.where(col >= blk, 1.0, allowed_row), 0.0)
            s = jnp.where(keep > 0.5, s, NEG_INF)
        else:
            bias_lo = (picked(2 * j) - 1.0) * (-NEG_INF)
            bias_hi = (picked(2 * j + 1) - 1.0) * (-NEG_INF)
            s = jnp.concatenate([s[:, :blk] + bias_lo, s[:, blk:] + bias_hi], axis=1)
        return _softmax_step(carry, s, v_ref[pl.ds(start, t), :])

    _, l, acc = _pipelined_causal_sweep(p, scores, consume, s_scr, _softmax_init(t, HEAD_DIM))
    o_ref[...] = (acc / l).astype(o_ref.dtype)


def _moba_attention(proj):
    b, s, _ = proj.shape
    t = 2 * MOBA_BLOCK
    nb = s // MOBA_BLOCK
    nb_pad = -(-nb // 8) * 8
    assert s % t == 0 and nb_pad <= LANES and nb > MOBA_TOPK
    qb, kb, vb = QM_OFF // HEAD_DIM, KM_OFF // HEAD_DIM, VM_OFF // HEAD_DIM
    return pl.pallas_call(
        _moba_kernel,
        grid=(b, MOBA_HEADS, s // t),
        in_specs=[pl.BlockSpec((None, t, HEAD_DIM), lambda bi, h, i: (bi, i, qb + h)),
                  pl.BlockSpec((None, s, HEAD_DIM), lambda bi, h, i: (bi, 0, kb + h)),
                  pl.BlockSpec((None, s, HEAD_DIM), lambda bi, h, i: (bi, 0, vb + h))],
        out_specs=pl.BlockSpec((None, t, HEAD_DIM), lambda bi, h, i: (bi, i, h)),
        out_shape=jax.ShapeDtypeStruct((b, s, MOBA_WIDTH), BF16),
        scratch_shapes=[pltpu.VMEM((nb_pad, HEAD_DIM), BF16), pltpu.VMEM((1, t, t), F32)],
        compiler_params=_params("arbitrary", "arbitrary", "arbitrary"),
        name="moba_attention",
    )(proj, proj, proj)


def _diff_kernel(q_ref, k_ref, v_ref, lq1_ref, lk1_ref, lq2_ref, lk2_ref, g_ref, o_ref, s_scr, *, lam_init):
    t = q_ref.shape[0]
    qi = pl.program_id(2)
    lam = (jnp.exp(jnp.sum(lq1_ref[...] * lk1_ref[...], axis=1, keepdims=True))
           - jnp.exp(jnp.sum(lq2_ref[...] * lk2_ref[...], axis=1, keepdims=True))
           + lam_init)
    q1 = q_ref[:, :HEAD_DIM]
    q2 = q_ref[:, HEAD_DIM:]

    def scores(j):
        start = pl.multiple_of(j * t, t)
        return (_qk(q1, k_ref[pl.ds(start, t), :HEAD_DIM]), _qk(q2, k_ref[pl.ds(start, t), HEAD_DIM:]))

    def consume(carry, j, s, diagonal):
        start = pl.multiple_of(j * t, t)
        v = v_ref[pl.ds(start, t), :]
        if diagonal:
            row, col = _row_col(t)
            s = tuple(jnp.where(col <= row, part, NEG_INF) for part in s)
        return tuple(_softmax_step(c, part, v) for c, part in zip(carry, s))

    init = (_softmax_init(t, DIFF_V_DIM), _softmax_init(t, DIFF_V_DIM))
    (_, l1, a1), (_, l2, a2) = _pipelined_causal_sweep(qi, scores, consume, s_scr, init)
    o = a1 / l1 - lam * (a2 / l2)
    o = _rms(o, g_ref[...]) * (1.0 - lam_init)
    o_ref[...] = o.astype(o_ref.dtype)


def _diff_attention(proj, lq1, lk1, lq2, lk2, g_sub, lam_init):
    b, s, _ = proj.shape
    t = ATT_T
    w = DIFF_V_DIM
    qb, kb, vb = QD_OFF // w, KD_OFF // w, VD_OFF // w
    vec = lambda width: pl.BlockSpec((1, width), lambda bi, h, i: (0, 0))
    return pl.pallas_call(
        functools.partial(_diff_kernel, lam_init=lam_init),
        grid=(b, DIFF_HEADS, s // t),
        in_specs=[pl.BlockSpec((None, t, w), lambda bi, h, i: (bi, i, qb + h)),
                  pl.BlockSpec((None, s, w), lambda bi, h, i: (bi, 0, kb + h)),
                  pl.BlockSpec((None, s, w), lambda bi, h, i: (bi, 0, vb + h)),
                  vec(HEAD_DIM), vec(HEAD_DIM), vec(HEAD_DIM), vec(HEAD_DIM), vec(w)],
        out_specs=pl.BlockSpec((None, t, w), lambda bi, h, i: (bi, i, h)),
        out_shape=jax.ShapeDtypeStruct((b, s, DIFF_WIDTH), BF16),
        scratch_shapes=[pltpu.VMEM((2, t, t), F32)],
        compiler_params=_params("arbitrary", "arbitrary", "arbitrary"),
        name="diff_attention",
    )(proj, proj, proj, lq1.reshape(1, -1), lk1.reshape(1, -1), lq2.reshape(1, -1),
      lk2.reshape(1, -1), g_sub.reshape(1, -1))


def _outproj_kernel(oa_ref, ob_ref, oc_ref, wa_ref, wb_ref, wc_ref, x_ref, o_ref, w_scr):
    @pl.when(pl.program_id(1) == 0)
    def _():
        _cast_rows(w_scr, wa_ref, 0)
        _cast_rows(w_scr, wb_ref, MOBA_WIDTH)
        _cast_rows(w_scr, wc_ref, MOBA_WIDTH + FOX_WIDTH)

    acc = jnp.dot(oa_ref[...], w_scr[:MOBA_WIDTH, :], preferred_element_type=F32)
    acc = acc + jnp.dot(ob_ref[...], w_scr[MOBA_WIDTH:MOBA_WIDTH + FOX_WIDTH, :],
                        preferred_element_type=F32)
    acc = acc + jnp.dot(oc_ref[...], w_scr[MOBA_WIDTH + FOX_WIDTH:, :], preferred_element_type=F32)
    o_ref[...] = x_ref[...] + acc


def _outproj(o_a, o_b, o_c, w_out, layer, x, tm=1024, tn=512):
    m, d = x.shape
    assert MOBA_WIDTH == FOX_WIDTH and (MOBA_WIDTH + FOX_WIDTH) % DIFF_WIDTH == 0
    return pl.pallas_call(
        _outproj_kernel,
        grid=(d // tn, m // tm),
        in_specs=[pl.BlockSpec((tm, MOBA_WIDTH), lambda n, i: (i, 0)),
                  pl.BlockSpec((tm, FOX_WIDTH), lambda n, i: (i, 0)),
                  pl.BlockSpec((tm, DIFF_WIDTH), lambda n, i: (i, 0)),
                  pl.BlockSpec((None, MOBA_WIDTH, tn), lambda n, i: (layer, 0, n)),
                  pl.BlockSpec((None, FOX_WIDTH, tn), lambda n, i: (layer, 1, n)),
                  pl.BlockSpec((None, DIFF_WIDTH, tn),
                               lambda n, i: (layer, (MOBA_WIDTH + FOX_WIDTH) // DIFF_WIDTH, n)),
                  pl.BlockSpec((tm, tn), lambda n, i: (i, n))],
        out_specs=pl.BlockSpec((tm, tn), lambda n, i: (i, n)),
        out_shape=jax.ShapeDtypeStruct((m, d), F32),
        scratch_shapes=[pltpu.VMEM((MOBA_WIDTH + FOX_WIDTH + DIFF_WIDTH, tn), BF16)],
        compiler_params=_params("arbitrary", "arbitrary"),
        name="outproj",
    )(o_a, o_b, o_c, w_out, w_out, w_out, x)


def _ffn_up_kernel(h_ref, wg_ref, wu_ref, wd_ref, o_ref, wd_out_ref, wg_scr, wu_scr, *, n_valid):
    n = pl.program_id(0)
    first_row_tile = pl.program_id(1) == 0

    @pl.when(first_row_tile & (n < n_valid))
    def _():
        _cast_rows(wg_scr, wg_ref)
        _cast_rows(wu_scr, wu_ref)
        wd_out_ref[...] = wd_ref[...].astype(wd_out_ref.dtype)

    @pl.when(first_row_tile & (n >= n_valid))
    def _():
        wd_out_ref[...] = jnp.zeros(wd_out_ref.shape, wd_out_ref.dtype)

    @pl.when(n < n_valid)
    def _():
        h = h_ref[...]
        g = jnp.dot(h, wg_scr[...], preferred_element_type=F32)
        u = jnp.dot(h, wu_scr[...], preferred_element_type=F32)
        o_ref[...] = (g / (1.0 + jnp.exp(-g)) * u).astype(o_ref.dtype)

    @pl.when(n >= n_valid)
    def _():
        o_ref[...] = jnp.zeros(o_ref.shape, o_ref.dtype)


def _ffn_up(h, w_gate, w_up, w_down, layer, tm=1024, tn=256):
    m, d = h.shape
    n_valid = FFN_DIM // tn
    assert FFN_DIM % tn == 0 and FFN_PAD % tn == 0
    last = lambda n: jnp.minimum(n, n_valid - 1)
    wspec = pl.BlockSpec((None, d, tn), lambda n, i: (layer, 0, last(n)))
    return pl.pallas_call(
        functools.partial(_ffn_up_kernel, n_valid=n_valid),
        grid=(FFN_PAD // tn, m // tm),
        in_specs=[pl.BlockSpec((tm, d), lambda n, i: (i, 0)), wspec, wspec,
                  pl.BlockSpec((None, tn, d), lambda n, i: (layer, last(n), 0))],
        out_specs=[pl.BlockSpec((tm, tn), lambda n, i: (i, n)),
                   pl.BlockSpec((tn, d), lambda n, i: (n, 0))],
        out_shape=[jax.ShapeDtypeStruct((m, FFN_PAD), BF16),
                   jax.ShapeDtypeStruct((FFN_PAD, d), BF16)],
        scratch_shapes=[pltpu.VMEM((d, tn), BF16), pltpu.VMEM((d, tn), BF16)],
        compiler_params=_params("arbitrary", "arbitrary"),
        name="ffn_up",
    )(h, w_gate, w_up, w_down)


def _ffn_down_kernel(a_ref, w_ref, x_ref, o_ref):
    @pl.when(pl.program_id(2) == 0)
    def _():
        o_ref[...] = x_ref[...]

    o_ref[...] += jnp.dot(a_ref[...], w_ref[...], preferred_element_type=F32)


def _ffn_down(a, wd, x, tm=1024, tn=1024, tk=2816):
    m, f = a.shape
    d = wd.shape[1]
    return pl.pallas_call(
        _ffn_down_kernel,
        grid=(m // tm, d // tn, f // tk),
        in_specs=[pl.BlockSpec((tm, tk), lambda i, j, k: (i, k)),
                  pl.BlockSpec((tk, tn), lambda i, j, k: (k, j)),
                  pl.BlockSpec((tm, tn), lambda i, j, k: (i, j))],
        out_specs=pl.BlockSpec((tm, tn), lambda i, j, k: (i, j)),
        out_shape=jax.ShapeDtypeStruct((m, d), F32),
        compiler_params=_params("arbitrary", "arbitrary", "arbitrary"),
        name="ffn_down",
    )(a, wd, x)


def _rotary_tables(seq):
    inv_freq = 1.0 / (ROPE_THETA ** (jnp.arange(0, HEAD_DIM, 2, dtype=F32) / HEAD_DIM))
    ang = jnp.arange(seq, dtype=F32)[:, None] * inv_freq[None, :]
    cos, sin = jnp.cos(ang), jnp.sin(ang)
    return jnp.concatenate([cos, cos], axis=-1), jnp.concatenate([-sin, sin], axis=-1)


def _column_gains(mq, mk, fq, fk, dq, dk):
    ones = lambda width: jnp.ones((width,), F32)
    rep = lambda g, width: jnp.tile(g.astype(F32), width // HEAD_DIM)
    qrep = lambda g, width: rep(g, width) * SCORE_SCALE
    return jnp.concatenate([
        qrep(mq, MOBA_WIDTH), rep(mk, MOBA_WIDTH), ones(MOBA_WIDTH),
        qrep(fq, FOX_WIDTH), rep(fk, FOX_WIDTH), ones(FOX_WIDTH),
        qrep(dq, DIFF_QK_WIDTH), rep(dk, DIFF_QK_WIDTH), ones(DIFF_WIDTH)]).reshape(1, PROJ_WIDTH)


def kernel(x, attn_norm, w_in, moba_q_norm, moba_k_norm, fox_q_norm, fox_k_norm, fox_forget_bias, diff_q_norm, diff_k_norm, diff_lambda_q1, diff_lambda_k1, diff_lambda_q2, diff_lambda_k2, diff_sub_norm, w_out, ffn_norm, w_gate, w_up, w_down):
    b, s, d = x.shape
    depth = w_in.shape[0]
    cos_full, sin_signed = _rotary_tables(s)
    x = x.reshape(b * s, d)
    w_in_t = _prep_w_in(w_in)
    for l in range(depth):
        gains = _column_gains(moba_q_norm[l], moba_k_norm[l], fox_q_norm[l], fox_k_norm[l],
                              diff_q_norm[l], diff_k_norm[l])
        bias = jnp.pad(fox_forget_bias[l].astype(F32), (0, LANES - FOX_HEADS)).reshape(1, LANES)

        h, flog = _rmsnorm_flog(x, attn_norm[l], w_in_t[l])
        proj = _inproj(h, w_in_t[l], gains, cos_full, sin_signed, s).reshape(b, s, PROJ_WIDTH)
        cum_col = _fox_prep(flog.reshape(b, s, LANES), bias)
        cum_row = jnp.transpose(cum_col[:, :, :FOX_HEADS], (0, 2, 1)).reshape(b, FOX_HEADS, 1, s)

        o_a = _moba_attention(proj)
        o_b = _fox_attention(proj, cum_col, cum_row)
        lam_init = 0.8 - 0.6 * math.exp(-0.3 * l)
        o_c = _diff_attention(proj, diff_lambda_q1[l], diff_lambda_k1[l], diff_lambda_q2[l],
                              diff_lambda_k2[l], diff_sub_norm[l], lam_init)
        x = _outproj(o_a.reshape(b * s, -1), o_b.reshape(b * s, -1), o_c.reshape(b * s, -1),
                     w_out, l, x)

        h = _rmsnorm(x, ffn_norm[l])
        a, w_d = _ffn_up(h, w_gate, w_up, w_down, l)
        x = _ffn_down(a, w_d, x)
    return x.reshape(b, s, d)
```

```python
import functools
import math

import jax
import jax.numpy as jnp
from jax import lax
from jax.experimental import pallas as pl
from jax.experimental.pallas import tpu as pltpu

D_MODEL = 4096
HEAD_DIM = 128
DIFF_V_DIM = 2 * HEAD_DIM
DIFF_HEADS = D_MODEL // (4 * DIFF_V_DIM)
MOBA_HEADS = (D_MODEL - DIFF_HEADS * DIFF_V_DIM) // (2 * HEAD_DIM)
FOX_HEADS = MOBA_HEADS
MOBA_WIDTH = MOBA_HEADS * HEAD_DIM
FOX_WIDTH = FOX_HEADS * HEAD_DIM
DIFF_QK_WIDTH = DIFF_HEADS * 2 * HEAD_DIM
DIFF_WIDTH = DIFF_HEADS * DIFF_V_DIM
FFN_DIM = -(-8 * D_MODEL // (3 * 256)) * 256
MOBA_BLOCK = 256
MOBA_TOPK = 3
ROPE_THETA = 10000.0
NORM_EPS = 1e-6
NEG_INF = -1e30
LOG2E = math.log2(math.e)

QM_OFF = 0
KM_OFF = QM_OFF + MOBA_WIDTH
VM_OFF = KM_OFF + MOBA_WIDTH
QF_OFF = VM_OFF + MOBA_WIDTH
KF_OFF = QF_OFF + FOX_WIDTH
VF_OFF = KF_OFF + FOX_WIDTH
FLOGIT_OFF = VF_OFF + FOX_WIDTH
QD_OFF = VF_OFF + FOX_WIDTH
KD_OFF = QD_OFF + DIFF_QK_WIDTH
VD_OFF = KD_OFF + DIFF_QK_WIDTH
PROJ_WIDTH = VD_OFF + DIFF_WIDTH

LANES = 128
MXU_DIM = 256
FFN_PAD = 11264
VMEM_LIMIT = 56 * 1024 * 1024

IN_TN = 512
ATT_T = 512
CAST_ROWS = 512

F32 = jnp.float32
BF16 = jnp.bfloat16


def _params(*sem):
    return pltpu.CompilerParams(dimension_semantics=sem, vmem_limit_bytes=VMEM_LIMIT)


def _cast_rows(dst_ref, src_ref, dst_row0=0):
    rows = src_ref.shape[0]
    step = math.gcd(rows, CAST_ROWS)
    for r in range(0, rows, step):
        dst_ref[dst_row0 + r:dst_row0 + r + step, :] = src_ref[r:r + step, :].astype(BF16)


def _rms(x, g):
    ms = jnp.mean(x * x, axis=-1, keepdims=True)
    return x * lax.rsqrt(ms + NORM_EPS) * g


def _rmsnorm_kernel(x_ref, g_ref, o_ref):
    o_ref[...] = _rms(x_ref[...], g_ref[...]).astype(o_ref.dtype)


def _rmsnorm(x, g, tm=256):
    m, d = x.shape
    return pl.pallas_call(
        _rmsnorm_kernel,
        grid=(m // tm,),
        in_specs=[pl.BlockSpec((tm, d), lambda i: (i, 0)),
                  pl.BlockSpec((1, d), lambda i: (0, 0))],
        out_specs=pl.BlockSpec((tm, d), lambda i: (i, 0)),
        out_shape=jax.ShapeDtypeStruct((m, d), BF16),
        compiler_params=_params("arbitrary"),
        name="rmsnorm",
    )(x, g.reshape(1, d))


def _rmsnorm_flog_kernel(x_ref, g_ref, w_ref, o_ref, flog_ref):
    h = _rms(x_ref[...], g_ref[...]).astype(BF16)
    o_ref[...] = h
    flog_ref[...] = _nt_dot(h, w_ref[...])


def _rmsnorm_flog(x, g, wt, tm=256):
    m, d = x.shape
    return pl.pallas_call(
        _rmsnorm_flog_kernel,
        grid=(m // tm,),
        in_specs=[pl.BlockSpec((tm, d), lambda i: (i, 0)),
                  pl.BlockSpec((1, d), lambda i: (0, 0)),
                  pl.BlockSpec((LANES, d), lambda i: (PROJ_WIDTH // LANES, 0))],
        out_specs=[pl.BlockSpec((tm, d), lambda i: (i, 0)),
                   pl.BlockSpec((tm, LANES), lambda i: (i, 0))],
        out_shape=[jax.ShapeDtypeStruct((m, d), BF16),
                   jax.ShapeDtypeStruct((m, LANES), F32)],
        compiler_params=_params("arbitrary"),
        name="rmsnorm_flog",
    )(x, g.reshape(1, d), wt)


WT_TR = 256
WT_ROWS = PROJ_WIDTH + WT_TR


def _prep_w_in_kernel(w_ref, *o_refs):
    i = pl.program_id(0)
    row = lax.broadcasted_iota(jnp.int32, (WT_TR, 1), 0)
    keep = jnp.where(i < PROJ_WIDTH // WT_TR, WT_TR, FOX_HEADS)
    for layer, o_ref in enumerate(o_refs):
        o_ref[...] = jnp.where(row < keep, w_ref[:, layer, :], 0.0).astype(o_ref.dtype)


def _prep_w_in(w_in):
    depth, d, width = w_in.shape
    wt = jnp.transpose(w_in, (2, 0, 1))
    n_main = PROJ_WIDTH // WT_TR
    assert FLOGIT_OFF % WT_TR == 0 and width == PROJ_WIDTH + FOX_HEADS

    def src_row(i):
        shifted = jnp.where(i >= FLOGIT_OFF // WT_TR, i * WT_TR + FOX_HEADS, i * WT_TR)
        return jnp.where(i >= n_main, FLOGIT_OFF, shifted)

    return pl.pallas_call(
        _prep_w_in_kernel,
        grid=(n_main + 1,),
        in_specs=[pl.BlockSpec((pl.Element(WT_TR), pl.Element(depth), pl.Element(d)),
                               lambda i: (src_row(i), 0, 0))],
        out_specs=[pl.BlockSpec((WT_TR, d), lambda i: (i, 0))] * depth,
        out_shape=[jax.ShapeDtypeStruct((WT_ROWS, d), BF16)] * depth,
        compiler_params=_params("arbitrary"),
        name="prep_w_in",
    )(wt)


def _nt_dot(a, b):
    return lax.dot_general(a, b, (((1,), (1,)), ((), ())), preferred_element_type=F32)


def _tile_kind(n):
    def in_seg(off, width):
        return (n >= off // IN_TN) & (n < (off + width) // IN_TN)
    rot = (in_seg(QM_OFF, 2 * MOBA_WIDTH) | in_seg(QD_OFF, 2 * DIFF_QK_WIDTH))
    norm = in_seg(QF_OFF, 2 * FOX_WIDTH)
    return rot, norm


IN_TILES_PER_STEP = 2


def _inproj_kernel(h_ref, w_ref, gain_ref, cos_ref, sin_ref, o_ref, y_scr):
    for sub in range(IN_TILES_PER_STEP):
        rot, norm = _tile_kind(IN_TILES_PER_STEP * pl.program_id(1) + sub)
        c0 = sub * IN_TN
        tile = slice(c0, c0 + IN_TN)
        heads = [(slice(lo, lo + HEAD_DIM), slice(c0 + lo, c0 + lo + HEAD_DIM))
                 for lo in range(0, IN_TN, HEAD_DIM)]

        @pl.when(jnp.logical_not(rot | norm))
        def _():
            o_ref[:, tile] = _nt_dot(h_ref[...], w_ref[tile, :]).astype(o_ref.dtype)

        @pl.when(norm)
        def _():
            acc = _nt_dot(h_ref[...], w_ref[tile, :])
            for loc, glob in heads:
                o_ref[:, glob] = _rms(acc[:, loc], gain_ref[:, glob]).astype(o_ref.dtype)

        @pl.when(rot)
        def _():
            acc = _nt_dot(h_ref[...], w_ref[tile, :])
            for loc, glob in heads:
                y_scr[:, loc] = _rms(acc[:, loc], gain_ref[:, glob])

        @pl.when(rot)
        def _():
            for loc, glob in heads:
                y = y_scr[:, loc]
                y = y * cos_ref[...] + pltpu.roll(y, HEAD_DIM // 2, 1) * sin_ref[...]
                o_ref[:, glob] = y.astype(o_ref.dtype)


def _inproj(h, wt, gain, cos_full, sin_signed, seq, tm=1024):
    m, d = h.shape
    tn = IN_TILES_PER_STEP * IN_TN
    assert PROJ_WIDTH % tn == 0
    pos_tiles = seq // tm
    return pl.pallas_call(
        _inproj_kernel,
        grid=(m // tm, PROJ_WIDTH // tn),
        in_specs=[pl.BlockSpec((tm, d), lambda i, n: (i, 0)),
                  pl.BlockSpec((tn, d), lambda i, n: (n, 0)),
                  pl.BlockSpec((1, tn), lambda i, n: (0, n)),
                  pl.BlockSpec((tm, HEAD_DIM), lambda i, n: (i % pos_tiles, 0)),
                  pl.BlockSpec((tm, HEAD_DIM), lambda i, n: (i % pos_tiles, 0))],
        out_specs=pl.BlockSpec((tm, tn), lambda i, n: (i, n)),
        out_shape=jax.ShapeDtypeStruct((m, PROJ_WIDTH), BF16),
        scratch_shapes=[pltpu.VMEM((tm, IN_TN), F32)],
        compiler_params=_params("arbitrary", "arbitrary"),
        name="inproj",
    )(h, wt, gain, cos_full, sin_signed)


def _fox_prep_kernel(fl_ref, b_ref, o_ref, *, chunk):
    seq = fl_ref.shape[0]
    row = lax.broadcasted_iota(jnp.int32, (chunk, chunk), 0)
    col = lax.broadcasted_iota(jnp.int32, (chunk, chunk), 1)
    tri = jnp.where(row >= col, 1.0, 0.0).astype(F32)

    def body(i, carry):
        start = pl.multiple_of(i * chunk, chunk)
        z = fl_ref[pl.ds(start, chunk), :] + b_ref[...]
        log_f = -(jnp.maximum(-z, 0.0) + jnp.log1p(jnp.exp(-jnp.abs(z))))
        c = jnp.dot(tri, log_f, precision=lax.Precision.HIGHEST,
                    preferred_element_type=F32) + carry
        o_ref[pl.ds(start, chunk), :] = c
        return c[chunk - 1:chunk, :]

    lax.fori_loop(0, seq // chunk, body, jnp.zeros((1, LANES), F32))


def _fox_prep(flog, bias, chunk=256):
    b, s, _ = flog.shape
    return pl.pallas_call(
        functools.partial(_fox_prep_kernel, chunk=chunk),
        grid=(b,),
        in_specs=[pl.BlockSpec((None, s, LANES), lambda i: (i, 0, 0)),
                  pl.BlockSpec((1, LANES), lambda i: (0, 0))],
        out_specs=pl.BlockSpec((None, s, LANES), lambda i: (i, 0, 0)),
        out_shape=jax.ShapeDtypeStruct((b, s, LANES), F32),
        compiler_params=_params("arbitrary"),
        name="fox_prep",
    )(flog, bias)


SCORE_SCALE = HEAD_DIM ** -0.5 * LOG2E


def _qk(q, k):
    return lax.dot_general(q, k, (((1,), (1,)), ((), ())), preferred_element_type=F32)


def _softmax_step(carry, s, v):
    m, l, acc = carry
    m_new = jnp.maximum(m, jnp.max(s, axis=1, keepdims=True))
    alpha = jnp.exp2(m - m_new)
    p = jnp.exp2(s - m_new)
    l = alpha * l + jnp.sum(p, axis=1, keepdims=True)
    acc = alpha * acc + jnp.dot(p.astype(BF16), v, preferred_element_type=F32)
    return m_new, l, acc


def _softmax_init(t, width):
    return (jnp.full((t, 1), NEG_INF, F32), jnp.zeros((t, 1), F32), jnp.zeros((t, width), F32))


def _row_col(t):
    return (lax.broadcasted_iota(jnp.int32, (t, t), 0), lax.broadcasted_iota(jnp.int32, (t, t), 1))


def _pipelined_causal_sweep(qi, scores, consume, s_scr, init):
    def put(s):
        for slot, part in enumerate(s):
            s_scr[slot] = part

    def get():
        return tuple(s_scr[slot] for slot in range(s_scr.shape[0]))

    put(scores(0))

    def pair(i, carry):
        j = 2 * i
        s0 = get()
        s1 = scores(j + 1)
        carry = consume(carry, j, s0, False)
        put(scores(j + 2))
        return consume(carry, j + 1, s1, False)

    def tail_one(carry):
        return consume(carry, qi, get(), True)

    def tail_two(carry):
        s_diag = scores(qi)
        carry = consume(carry, qi - 1, get(), False)
        return consume(carry, qi, s_diag, True)

    carry = lax.fori_loop(0, qi // 2, pair, init)
    return lax.cond(qi % 2 == 1, tail_two, tail_one, carry)


def _wide_causal_sweep(p, scores, consume, s_scr, init):
    def put(s):
        for slot, part in enumerate(s):
            s_scr[slot] = part

    def get():
        return tuple(s_scr[slot] for slot in range(s_scr.shape[0]))

    put(scores(0))

    def pair(i, carry):
        j = 2 * i
        s0 = get()
        s1 = scores(j + 1)
        carry = consume(carry, j, s0, False)
        put(scores(j + 2))
        return consume(carry, j + 1, s1, False)

    carry = lax.fori_loop(0, p, pair, init)
    s_last = scores(2 * p + 1)
    carry = consume(carry, 2 * p, get(), True)
    return consume(carry, 2 * p + 1, s_last, True)


def _fox_kernel(q_ref, k_ref, v_ref, cc_ref, cr_ref, o_ref, s_scr):
    tq = q_ref.shape[0]
    tk = s_scr.shape[-1]
    h = pl.program_id(1)
    p = pl.program_id(2)
    q = q_ref[...]
    lane = lax.broadcasted_iota(jnp.int32, (tq, LANES), 1)
    cq = jnp.sum(jnp.where(lane == h, cc_ref[...], 0.0), axis=1, keepdims=True) * LOG2E

    def scores(j):
        start = pl.multiple_of(j * tk, tk)
        ck = cr_ref[:, pl.ds(start, tk)] * LOG2E
        return (_qk(q, k_ref[pl.ds(start, tk), :]) + cq - ck,)

    def consume(carry, j, s, diagonal):
        start = pl.multiple_of(j * tk, tk)
        s, = s
        if diagonal:
            row = lax.broadcasted_iota(jnp.int32, (tq, tk), 0)
            col = lax.broadcasted_iota(jnp.int32, (tq, tk), 1) + (j - 2 * p) * tk
            s = jnp.where(col <= row, s, NEG_INF)
        return _softmax_step(carry, s, v_ref[pl.ds(start, tk), :])

    _, l, acc = _wide_causal_sweep(p, scores, consume, s_scr, _softmax_init(tq, HEAD_DIM))
    o_ref[...] = (acc / l).astype(o_ref.dtype)


def _fox_attention(proj, cum_col, cum_row):
    b, s, _ = proj.shape
    t = 2 * ATT_T
    qb, kb, vb = QF_OFF // HEAD_DIM, KF_OFF // HEAD_DIM, VF_OFF // HEAD_DIM
    return pl.pallas_call(
        _fox_kernel,
        grid=(b, FOX_HEADS, s // t),
        in_specs=[pl.BlockSpec((None, t, HEAD_DIM), lambda bi, h, i: (bi, i, qb + h)),
                  pl.BlockSpec((None, s, HEAD_DIM), lambda bi, h, i: (bi, 0, kb + h)),
                  pl.BlockSpec((None, s, HEAD_DIM), lambda bi, h, i: (bi, 0, vb + h)),
                  pl.BlockSpec((None, t, LANES), lambda bi, h, i: (bi, i, 0)),
                  pl.BlockSpec((None, None, 1, s), lambda bi, h, i: (bi, h, 0, 0))],
        out_specs=pl.BlockSpec((None, t, HEAD_DIM), lambda bi, h, i: (bi, i, h)),
        out_shape=jax.ShapeDtypeStruct((b, s, FOX_WIDTH), BF16),
        scratch_shapes=[pltpu.VMEM((1, t, ATT_T), F32)],
        compiler_params=_params("arbitrary", "arbitrary", "arbitrary"),
        name="fox_attention",
    )(proj, proj, proj, cum_col, cum_row)


def _moba_kernel(q_ref, k_ref, v_ref, o_ref, kmean_ref, s_scr):
    t = q_ref.shape[0]
    blk = MOBA_BLOCK
    seq = k_ref.shape[0]
    nb = seq // blk
    nb_pad = kmean_ref.shape[0]
    p = pl.program_id(2)

    @pl.when(p == 0)
    def _():
        bidx = lax.broadcasted_iota(jnp.int32, (nb_pad, seq), 0)
        pos = lax.broadcasted_iota(jnp.int32, (nb_pad, seq), 1)
        member = jnp.where(lax.shift_right_logical(pos, int(math.log2(blk))) == bidx, 1.0, 0.0)
        ksum = jnp.dot(member.astype(BF16), k_ref[...], preferred_element_type=F32)
        kmean_ref[...] = (ksum * (1.0 / blk)).astype(BF16)

    q = q_ref[...]

    bj = lax.broadcasted_iota(jnp.int32, (nb_pad, t), 0)
    qcol = lax.broadcasted_iota(jnp.int32, (nb_pad, t), 1)
    own = 2 * p + jnp.where(qcol >= blk, 1, 0)
    past = bj < own
    gate = jnp.where(past, _qk(kmean_ref[...], q), NEG_INF)
    rank = jnp.zeros((nb_pad, t), F32)
    for jp in range(nb):
        c = gate[jp:jp + 1, :]
        tie = jnp.where(bj > jp, 1.0, 0.0)
        rank = rank + jnp.where(c > gate, 1.0, jnp.where(c == gate, tie, 0.0))
    chosen_t = jnp.where(past, jnp.where(rank < float(MOBA_TOPK), 1.0, 0.0), 0.0)
    pad = jnp.zeros((LANES - nb_pad, t), F32)
    chosen = jnp.transpose(jnp.concatenate([chosen_t, pad], axis=0))
    lane = lax.broadcasted_iota(jnp.int32, (t, LANES), 1)

    def picked(j):
        return jnp.sum(jnp.where(lane == j, chosen, 0.0), axis=1, keepdims=True)

    def scores(j):
        start = pl.multiple_of(j * t, t)
        return (_qk(q, k_ref[pl.ds(start, t), :]),)

    def consume(carry, j, s, diagonal):
        start = pl.multiple_of(j * t, t)
        s, = s
        if diagonal:
            row, col = _row_col(t)
            rcol = lax.broadcasted_iota(jnp.int32, (t, 1), 0)
            allowed_row = jnp.where(rcol < blk, 1.0, picked(2 * j))
            keep = jnp.where(col <= row, jnp.where(col >= blk, 1.0, allowed_row), 0.0)
            s = jnp.where(keep > 0.5, s, NEG_INF)
        else:
            bias_lo = (picked(2 * j) - 1.0) * (-NEG_INF)
            bias_hi = (picked(2 * j + 1) - 1.0) * (-NEG_INF)
            s = jnp.concatenate([s[:, :blk] + bias_lo, s[:, blk:] + bias_hi], axis=1)
        return _softmax_step(carry, s, v_ref[pl.ds(start, t), :])

    _, l, acc = _pipelined_causal_sweep(p, scores, consume, s_scr, _softmax_init(t, HEAD_DIM))
    o_ref[...] = (acc / l).astype(o_ref.dtype)


def _moba_attention(proj):
    b, s, _ = proj.shape
    t = 2 * MOBA_BLOCK
    nb = s // MOBA_BLOCK
    nb_pad = -(-nb // 8) * 8
    assert s % t == 0 and nb_pad <= LANES and nb > MOBA_TOPK
    qb, kb, vb = QM_OFF // HEAD_DIM, KM_OFF // HEAD_DIM, VM_OFF // HEAD_DIM
    return pl.pallas_call(
        _moba_kernel,
        grid=(b, MOBA_HEADS, s // t),
        in_specs=[pl.BlockSpec((None, t, HEAD_DIM), lambda bi, h, i: (bi, i, qb + h)),
                  pl.BlockSpec((None, s, HEAD_DIM), lambda bi, h, i: (bi, 0, kb + h)),
                  pl.BlockSpec((None, s, HEAD_DIM), lambda bi, h, i: (bi, 0, vb + h))],
        out_specs=pl.BlockSpec((None, t, HEAD_DIM), lambda bi, h, i: (bi, i, h)),
        out_shape=jax.ShapeDtypeStruct((b, s, MOBA_WIDTH), BF16),
        scratch_shapes=[pltpu.VMEM((nb_pad, HEAD_DIM), BF16), pltpu.VMEM((1, t, t), F32)],
        compiler_params=_params("arbitrary", "arbitrary", "arbitrary"),
        name="moba_attention",
    )(proj, proj, proj)


def _diff_kernel(q_ref, k_ref, v_ref, lq1_ref, lk1_ref, lq2_ref, lk2_ref, g_ref, o_ref, s_scr, *, lam_init):
    t = q_ref.shape[0]
    tk = s_scr.shape[-1]
    p = pl.program_id(2)
    lam = (jnp.exp(jnp.sum(lq1_ref[...] * lk1_ref[...], axis=1, keepdims=True))
           - jnp.exp(jnp.sum(lq2_ref[...] * lk2_ref[...], axis=1, keepdims=True))
           + lam_init)
    q1 = q_ref[:, :HEAD_DIM]
    q2 = q_ref[:, HEAD_DIM:]

    def scores(j):
        start = pl.multiple_of(j * tk, tk)
        return (_qk(q1, k_ref[pl.ds(start, tk), :HEAD_DIM]), _qk(q2, k_ref[pl.ds(start, tk), HEAD_DIM:]))

    def consume(carry, j, s, diagonal):
        start = pl.multiple_of(j * tk, tk)
        v = v_ref[pl.ds(start, tk), :]
        if diagonal:
            row = lax.broadcasted_iota(jnp.int32, (t, tk), 0)
            col = lax.broadcasted_iota(jnp.int32, (t, tk), 1) + (j - 2 * p) * tk
            s = tuple(jnp.where(col <= row, part, NEG_INF) for part in s)
        return tuple(_softmax_step(c, part, v) for c, part in zip(carry, s))

    init = (_softmax_init(t, DIFF_V_DIM), _softmax_init(t, DIFF_V_DIM))
    (_, l1, a1), (_, l2, a2) = _wide_causal_sweep(p, scores, consume, s_scr, init)
    o = a1 / l1 - lam * (a2 / l2)
    o = _rms(o, g_ref[...]) * (1.0 - lam_init)
    o_ref[...] = o.astype(o_ref.dtype)


def _diff_attention(proj, lq1, lk1, lq2, lk2, g_sub, lam_init):
    b, s, _ = proj.shape
    t = 2 * ATT_T
    w = DIFF_V_DIM
    qb, kb, vb = QD_OFF // w, KD_OFF // w, VD_OFF // w
    vec = lambda width: pl.BlockSpec((1, width), lambda bi, h, i: (0, 0))
    return pl.pallas_call(
        functools.partial(_diff_kernel, lam_init=lam_init),
        grid=(b, DIFF_HEADS, s // t),
        in_specs=[pl.BlockSpec((None, t, w), lambda bi, h, i: (bi, i, qb + h)),
                  pl.BlockSpec((None, s, w), lambda bi, h, i: (bi, 0, kb + h)),
                  pl.BlockSpec((None, s, w), lambda bi, h, i: (bi, 0, vb + h)),
                  vec(HEAD_DIM), vec(HEAD_DIM), vec(HEAD_DIM), vec(HEAD_DIM), vec(w)],
        out_specs=pl.BlockSpec((None, t, w), lambda bi, h, i: (bi, i, h)),
        out_shape=jax.ShapeDtypeStruct((b, s, DIFF_WIDTH), BF16),
        scratch_shapes=[pltpu.VMEM((2, t, ATT_T), F32)],
        compiler_params=_params("arbitrary", "arbitrary", "arbitrary"),
        name="diff_attention",
    )(proj, proj, proj, lq1.reshape(1, -1), lk1.reshape(1, -1), lq2.reshape(1, -1),
      lk2.reshape(1, -1), g_sub.reshape(1, -1))


def _outproj_kernel(oa_ref, ob_ref, oc_ref, wa_ref, wb_ref, wc_ref, x_ref, o_ref, w_scr):
    @pl.when(pl.program_id(1) == 0)
    def _():
        _cast_rows(w_scr, wa_ref, 0)
        _cast_rows(w_scr, wb_ref, MOBA_WIDTH)
        _cast_rows(w_scr, wc_ref, MOBA_WIDTH + FOX_WIDTH)

    acc = jnp.dot(oa_ref[...], w_scr[:MOBA_WIDTH, :], preferred_element_type=F32)
    acc = acc + jnp.dot(ob_ref[...], w_scr[MOBA_WIDTH:MOBA_WIDTH + FOX_WIDTH, :],
                        preferred_element_type=F32)
    acc = acc + jnp.dot(oc_ref[...], w_scr[MOBA_WIDTH + FOX_WIDTH:, :], preferred_element_type=F32)
    o_ref[...] = x_ref[...] + acc


def _outproj(o_a, o_b, o_c, w_out, layer, x, tm=1024, tn=512):
    m, d = x.shape
    assert MOBA_WIDTH == FOX_WIDTH and (MOBA_WIDTH + FOX_WIDTH) % DIFF_WIDTH == 0
    return pl.pallas_call(
        _outproj_kernel,
        grid=(d // tn, m // tm),
        in_specs=[pl.BlockSpec((tm, MOBA_WIDTH), lambda n, i: (i, 0)),
                  pl.BlockSpec((tm, FOX_WIDTH), lambda n, i: (i, 0)),
                  pl.BlockSpec((tm, DIFF_WIDTH), lambda n, i: (i, 0)),
                  pl.BlockSpec((None, MOBA_WIDTH, tn), lambda n, i: (layer, 0, n)),
                  pl.BlockSpec((None, FOX_WIDTH, tn), lambda n, i: (layer, 1, n)),
                  pl.BlockSpec((None, DIFF_WIDTH, tn),
                               lambda n, i: (layer, (MOBA_WIDTH + FOX_WIDTH) // DIFF_WIDTH, n)),
                  pl.BlockSpec((tm, tn), lambda n, i: (i, n))],
        out_specs=pl.BlockSpec((tm, tn), lambda n, i: (i, n)),
        out_shape=jax.ShapeDtypeStruct((m, d), F32),
        scratch_shapes=[pltpu.VMEM((MOBA_WIDTH + FOX_WIDTH + DIFF_WIDTH, tn), BF16)],
        compiler_params=_params("arbitrary", "arbitrary"),
        name="outproj",
    )(o_a, o_b, o_c, w_out, w_out, w_out, x)


def _ffn_up_kernel(h_ref, wg_ref, wu_ref, o_ref, wg_scr, wu_scr, *, n_valid):
    n = pl.program_id(0)

    @pl.when((pl.program_id(1) == 0) & (n < n_valid))
    def _():
        _cast_rows(wg_scr, wg_ref)
        _cast_rows(wu_scr, wu_ref)

    @pl.when(n < n_valid)
    def _():
        h = h_ref[...]
        g = jnp.dot(h, wg_scr[...], preferred_element_type=F32)
        u = jnp.dot(h, wu_scr[...], preferred_element_type=F32)
        o_ref[...] = (g / (1.0 + jnp.exp(-g)) * u).astype(o_ref.dtype)

    @pl.when(n >= n_valid)
    def _():
        o_ref[...] = jnp.zeros(o_ref.shape, o_ref.dtype)


def _ffn_up(h, w_gate, w_up, layer, tm=1024, tn=256):
    m, d = h.shape
    n_valid = FFN_DIM // tn
    assert FFN_DIM % tn == 0 and FFN_PAD % tn == 0
    wspec = pl.BlockSpec((None, d, tn), lambda n, i: (layer, 0, jnp.minimum(n, n_valid - 1)))
    return pl.pallas_call(
        functools.partial(_ffn_up_kernel, n_valid=n_valid),
        grid=(FFN_PAD // tn, m // tm),
        in_specs=[pl.BlockSpec((tm, d), lambda n, i: (i, 0)), wspec, wspec],
        out_specs=pl.BlockSpec((tm, tn), lambda n, i: (i, n)),
        out_shape=jax.ShapeDtypeStruct((m, FFN_PAD), BF16),
        scratch_shapes=[pltpu.VMEM((d, tn), BF16), pltpu.VMEM((d, tn), BF16)],
        compiler_params=_params("arbitrary", "arbitrary"),
        name="ffn_up",
    )(h, w_gate, w_up)


def _cast_pad_kernel(w_ref, o_ref, *, n_valid):
    @pl.when(pl.program_id(0) < n_valid)
    def _():
        o_ref[...] = w_ref[...].astype(o_ref.dtype)

    @pl.when(pl.program_id(0) >= n_valid)
    def _():
        o_ref[...] = jnp.zeros(o_ref.shape, o_ref.dtype)


def _cast_pad_rows(w, layer, rows_out, tr=256):
    _, rows, cols = w.shape
    n_valid = rows // tr
    assert rows % tr == 0 and rows_out % tr == 0
    return pl.pallas_call(
        functools.partial(_cast_pad_kernel, n_valid=n_valid),
        grid=(rows_out // tr,),
        in_specs=[pl.BlockSpec((None, tr, cols), lambda i: (layer, jnp.minimum(i, n_valid - 1), 0))],
        out_specs=pl.BlockSpec((tr, cols), lambda i: (i, 0)),
        out_shape=jax.ShapeDtypeStruct((rows_out, cols), BF16),
        compiler_params=_params("arbitrary"),
        name="cast_pad_rows",
    )(w)


def _ffn_down_kernel(a_ref, w_ref, x_ref, o_ref):
    @pl.when(pl.program_id(2) == 0)
    def _():
        o_ref[...] = x_ref[...]

    o_ref[...] += jnp.dot(a_ref[...], w_ref[...], preferred_element_type=F32)


def _ffn_down(a, wd, x, tm=1024, tn=1024, tk=2816):
    m, f = a.shape
    d = wd.shape[1]
    return pl.pallas_call(
        _ffn_down_kernel,
        grid=(m // tm, d // tn, f // tk),
        in_specs=[pl.BlockSpec((tm, tk), lambda i, j, k: (i, k)),
                  pl.BlockSpec((tk, tn), lambda i, j, k: (k, j)),
                  pl.BlockSpec((tm, tn), lambda i, j, k: (i, j))],
        out_specs=pl.BlockSpec((tm, tn), lambda i, j, k: (i, j)),
        out_shape=jax.ShapeDtypeStruct((m, d), F32),
        compiler_params=_params("arbitrary", "arbitrary", "arbitrary"),
        name="ffn_down",
    )(a, wd, x)


def _rotary_tables(seq):
    inv_freq = 1.0 / (ROPE_THETA ** (jnp.arange(0, HEAD_DIM, 2, dtype=F32) / HEAD_DIM))
    ang = jnp.arange(seq, dtype=F32)[:, None] * inv_freq[None, :]
    cos, sin = jnp.cos(ang), jnp.sin(ang)
    return jnp.concatenate([cos, cos], axis=-1), jnp.concatenate([-sin, sin], axis=-1)


def _column_gains(mq, mk, fq, fk, dq, dk):
    ones = lambda width: jnp.ones((width,), F32)
    rep = lambda g, width: jnp.tile(g.astype(F32), width // HEAD_DIM)
    qrep = lambda g, width: rep(g, width) * SCORE_SCALE
    return jnp.concatenate([
        qrep(mq, MOBA_WIDTH), rep(mk, MOBA_WIDTH), ones(MOBA_WIDTH),
        qrep(fq, FOX_WIDTH), rep(fk, FOX_WIDTH), ones(FOX_WIDTH),
        qrep(dq, DIFF_QK_WIDTH), rep(dk, DIFF_QK_WIDTH), ones(DIFF_WIDTH)]).reshape(1, PROJ_WIDTH)


def kernel(x, attn_norm, w_in, moba_q_norm, moba_k_norm, fox_q_norm, fox_k_norm, fox_forget_bias, diff_q_norm, diff_k_norm, diff_lambda_q1, diff_lambda_k1, diff_lambda_q2, diff_lambda_k2, diff_sub_norm, w_out, ffn_norm, w_gate, w_up, w_down):
    b, s, d = x.shape
    depth = w_in.shape[0]
    cos_full, sin_signed = _rotary_tables(s)
    x = x.reshape(b * s, d)
    w_in_t = _prep_w_in(w_in)
    for l in range(depth):
        gains = _column_gains(moba_q_norm[l], moba_k_norm[l], fox_q_norm[l], fox_k_norm[l],
                              diff_q_norm[l], diff_k_norm[l])
        bias = jnp.pad(fox_forget_bias[l].astype(F32), (0, LANES - FOX_HEADS)).reshape(1, LANES)

        h, flog = _rmsnorm_flog(x, attn_norm[l], w_in_t[l])
        proj = _inproj(h, w_in_t[l], gains, cos_full, sin_signed, s).reshape(b, s, PROJ_WIDTH)
        cum_col = _fox_prep(flog.reshape(b, s, LANES), bias)
        cum_row = jnp.transpose(cum_col[:, :, :FOX_HEADS], (0, 2, 1)).reshape(b, FOX_HEADS, 1, s)

        o_a = _moba_attention(proj)
        o_b = _fox_attention(proj, cum_col, cum_row)
        lam_init = 0.8 - 0.6 * math.exp(-0.3 * l)
        o_c = _diff_attention(proj, diff_lambda_q1[l], diff_lambda_k1[l], diff_lambda_q2[l],
                              diff_lambda_k2[l], diff_sub_norm[l], lam_init)
        x = _outproj(o_a.reshape(b * s, -1), o_b.reshape(b * s, -1), o_c.reshape(b * s, -1),
                     w_out, l, x)

        h = _rmsnorm(x, ffn_norm[l])
        a = _ffn_up(h, w_gate, w_up, l)
        x = _ffn_down(a, _cast_pad_rows(w_down, l, FFN_PAD), x)
    return x.reshape(b, s, d)
```

```python
import functools
import math

import jax
import jax.numpy as jnp
from jax import lax
from jax.experimental import pallas as pl
from jax.experimental.pallas import tpu as pltpu

D_MODEL = 4096
HEAD_DIM = 128
DIFF_V_DIM = 2 * HEAD_DIM
DIFF_HEADS = D_MODEL // (4 * DIFF_V_DIM)
MOBA_HEADS = (D_MODEL - DIFF_HEADS * DIFF_V_DIM) // (2 * HEAD_DIM)
FOX_HEADS = MOBA_HEADS
MOBA_WIDTH = MOBA_HEADS * HEAD_DIM
FOX_WIDTH = FOX_HEADS * HEAD_DIM
DIFF_QK_WIDTH = DIFF_HEADS * 2 * HEAD_DIM
DIFF_WIDTH = DIFF_HEADS * DIFF_V_DIM
FFN_DIM = -(-8 * D_MODEL // (3 * 256)) * 256
MOBA_BLOCK = 256
MOBA_TOPK = 3
ROPE_THETA = 10000.0
NORM_EPS = 1e-6
NEG_INF = -1e30
LOG2E = math.log2(math.e)

QM_OFF = 0
KM_OFF = QM_OFF + MOBA_WIDTH
VM_OFF = KM_OFF + MOBA_WIDTH
QF_OFF = VM_OFF + MOBA_WIDTH
KF_OFF = QF_OFF + FOX_WIDTH
VF_OFF = KF_OFF + FOX_WIDTH
FLOGIT_OFF = VF_OFF + FOX_WIDTH
QD_OFF = VF_OFF + FOX_WIDTH
KD_OFF = QD_OFF + DIFF_QK_WIDTH
VD_OFF = KD_OFF + DIFF_QK_WIDTH
PROJ_WIDTH = VD_OFF + DIFF_WIDTH

LANES = 128
FFN_PAD = -(-FFN_DIM // 1024) * 1024
VMEM_LIMIT = 56 * 1024 * 1024

IN_TN = 512
ATT_T = 512
CAST_ROWS = 512

F32 = jnp.float32
BF16 = jnp.bfloat16


def _params(*sem):
    return pltpu.CompilerParams(dimension_semantics=sem, vmem_limit_bytes=VMEM_LIMIT)


def _cast_rows(dst_ref, src_ref, dst_row0=0):
    rows = src_ref.shape[0]
    step = math.gcd(rows, CAST_ROWS)
    for r in range(0, rows, step):
        dst_ref[dst_row0 + r:dst_row0 + r + step, :] = src_ref[r:r + step, :].astype(BF16)


def _rms(x, g):
    ms = jnp.mean(x * x, axis=-1, keepdims=True)
    return x * lax.rsqrt(ms + NORM_EPS) * g


def _rmsnorm_kernel(x_ref, g_ref, o_ref):
    o_ref[...] = _rms(x_ref[...], g_ref[...]).astype(o_ref.dtype)


def _rmsnorm(x, g, tm=256):
    m, d = x.shape
    return pl.pallas_call(
        _rmsnorm_kernel,
        grid=(m // tm,),
        in_specs=[pl.BlockSpec((tm, d), lambda i: (i, 0)),
                  pl.BlockSpec((1, d), lambda i: (0, 0))],
        out_specs=pl.BlockSpec((tm, d), lambda i: (i, 0)),
        out_shape=jax.ShapeDtypeStruct((m, d), BF16),
        compiler_params=_params("arbitrary"),
        name="rmsnorm",
    )(x, g.reshape(1, d))


def _rmsnorm_flog_kernel(x_ref, g_ref, w_ref, o_ref, flog_ref):
    h = _rms(x_ref[...], g_ref[...]).astype(BF16)
    o_ref[...] = h
    flog_ref[...] = _nt_dot(h, w_ref[...])


def _rmsnorm_flog(x, g, wt, tm=256):
    m, d = x.shape
    return pl.pallas_call(
        _rmsnorm_flog_kernel,
        grid=(m // tm,),
        in_specs=[pl.BlockSpec((tm, d), lambda i: (i, 0)),
                  pl.BlockSpec((1, d), lambda i: (0, 0)),
                  pl.BlockSpec((LANES, d), lambda i: (PROJ_WIDTH // LANES, 0))],
        out_specs=[pl.BlockSpec((tm, d), lambda i: (i, 0)),
                   pl.BlockSpec((tm, LANES), lambda i: (i, 0))],
        out_shape=[jax.ShapeDtypeStruct((m, d), BF16),
                   jax.ShapeDtypeStruct((m, LANES), F32)],
        compiler_params=_params("arbitrary"),
        name="rmsnorm_flog",
    )(x, g.reshape(1, d), wt)


WT_TR = 256
WT_ROWS = PROJ_WIDTH + WT_TR


def _prep_w_in_kernel(w_ref, *o_refs):
    i = pl.program_id(0)
    row = lax.broadcasted_iota(jnp.int32, (WT_TR, 1), 0)
    keep = jnp.where(i < PROJ_WIDTH // WT_TR, WT_TR, FOX_HEADS)
    for layer, o_ref in enumerate(o_refs):
        o_ref[...] = jnp.where(row < keep, w_ref[:, layer, :], 0.0).astype(o_ref.dtype)


def _prep_w_in(w_in):
    depth, d, width = w_in.shape
    wt = jnp.transpose(w_in, (2, 0, 1))
    n_main = PROJ_WIDTH // WT_TR
    assert FLOGIT_OFF % WT_TR == 0 and width == PROJ_WIDTH + FOX_HEADS

    def src_row(i):
        shifted = jnp.where(i >= FLOGIT_OFF // WT_TR, i * WT_TR + FOX_HEADS, i * WT_TR)
        return jnp.where(i >= n_main, FLOGIT_OFF, shifted)

    return pl.pallas_call(
        _prep_w_in_kernel,
        grid=(n_main + 1,),
        in_specs=[pl.BlockSpec((pl.Element(WT_TR), pl.Element(depth), pl.Element(d)),
                               lambda i: (src_row(i), 0, 0))],
        out_specs=[pl.BlockSpec((WT_TR, d), lambda i: (i, 0))] * depth,
        out_shape=[jax.ShapeDtypeStruct((WT_ROWS, d), BF16)] * depth,
        compiler_params=_params("arbitrary"),
        name="prep_w_in",
    )(wt)


def _nt_dot(a, b):
    return lax.dot_general(a, b, (((1,), (1,)), ((), ())), preferred_element_type=F32)


def _tile_kind(n):
    def in_seg(off, width):
        return (n >= off // IN_TN) & (n < (off + width) // IN_TN)
    rot = (in_seg(QM_OFF, 2 * MOBA_WIDTH) | in_seg(QD_OFF, 2 * DIFF_QK_WIDTH))
    norm = in_seg(QF_OFF, 2 * FOX_WIDTH)
    return rot, norm


IN_TILES_PER_STEP = 2


def _inproj_kernel(h_ref, w_ref, gain_ref, cos_ref, sin_ref, o_ref, y_scr):
    for sub in range(IN_TILES_PER_STEP):
        rot, norm = _tile_kind(IN_TILES_PER_STEP * pl.program_id(1) + sub)
        c0 = sub * IN_TN
        tile = slice(c0, c0 + IN_TN)
        heads = [(slice(lo, lo + HEAD_DIM), slice(c0 + lo, c0 + lo + HEAD_DIM))
                 for lo in range(0, IN_TN, HEAD_DIM)]

        @pl.when(jnp.logical_not(rot | norm))
        def _():
            o_ref[:, tile] = _nt_dot(h_ref[...], w_ref[tile, :]).astype(o_ref.dtype)

        @pl.when(norm)
        def _():
            acc = _nt_dot(h_ref[...], w_ref[tile, :])
            for loc, glob in heads:
                o_ref[:, glob] = _rms(acc[:, loc], gain_ref[:, glob]).astype(o_ref.dtype)

        @pl.when(rot)
        def _():
            acc = _nt_dot(h_ref[...], w_ref[tile, :])
            for loc, glob in heads:
                y_scr[:, loc] = _rms(acc[:, loc], gain_ref[:, glob])

        @pl.when(rot)
        def _():
            for loc, glob in heads:
                y = y_scr[:, loc]
                y = y * cos_ref[...] + pltpu.roll(y, HEAD_DIM // 2, 1) * sin_ref[...]
                o_ref[:, glob] = y.astype(o_ref.dtype)


def _inproj(h, wt, gain, cos_full, sin_signed, seq, tm=1024):
    m, d = h.shape
    tn = IN_TILES_PER_STEP * IN_TN
    assert PROJ_WIDTH % tn == 0
    pos_tiles = seq // tm
    return pl.pallas_call(
        _inproj_kernel,
        grid=(m // tm, PROJ_WIDTH // tn),
        in_specs=[pl.BlockSpec((tm, d), lambda i, n: (i, 0)),
                  pl.BlockSpec((tn, d), lambda i, n: (n, 0)),
                  pl.BlockSpec((1, tn), lambda i, n: (0, n)),
                  pl.BlockSpec((tm, HEAD_DIM), lambda i, n: (i % pos_tiles, 0)),
                  pl.BlockSpec((tm, HEAD_DIM), lambda i, n: (i % pos_tiles, 0))],
        out_specs=pl.BlockSpec((tm, tn), lambda i, n: (i, n)),
        out_shape=jax.ShapeDtypeStruct((m, PROJ_WIDTH), BF16),
        scratch_shapes=[pltpu.VMEM((tm, IN_TN), F32)],
        compiler_params=_params("arbitrary", "arbitrary"),
        name="inproj",
    )(h, wt, gain, cos_full, sin_signed)


def _fox_prep_kernel(fl_ref, b_ref, o_ref, *, chunk):
    seq = fl_ref.shape[0]
    row = lax.broadcasted_iota(jnp.int32, (chunk, chunk), 0)
    col = lax.broadcasted_iota(jnp.int32, (chunk, chunk), 1)
    tri = jnp.where(row >= col, 1.0, 0.0).astype(F32)

    def body(i, carry):
        start = pl.multiple_of(i * chunk, chunk)
        z = fl_ref[pl.ds(start, chunk), :] + b_ref[...]
        log_f = -(jnp.maximum(-z, 0.0) + jnp.log1p(jnp.exp(-jnp.abs(z))))
        c = jnp.dot(tri, log_f, precision=lax.Precision.HIGHEST,
                    preferred_element_type=F32) + carry
        o_ref[pl.ds(start, chunk), :] = c
        return c[chunk - 1:chunk, :]

    lax.fori_loop(0, seq // chunk, body, jnp.zeros((1, LANES), F32))


def _fox_prep(flog, bias, chunk=256):
    b, s, _ = flog.shape
    return pl.pallas_call(
        functools.partial(_fox_prep_kernel, chunk=chunk),
        grid=(b,),
        in_specs=[pl.BlockSpec((None, s, LANES), lambda i: (i, 0, 0)),
                  pl.BlockSpec((1, LANES), lambda i: (0, 0))],
        out_specs=pl.BlockSpec((None, s, LANES), lambda i: (i, 0, 0)),
        out_shape=jax.ShapeDtypeStruct((b, s, LANES), F32),
        compiler_params=_params("arbitrary"),
        name="fox_prep",
    )(flog, bias)


SCORE_SCALE = HEAD_DIM ** -0.5 * LOG2E


def _qk(q, k):
    return lax.dot_general(q, k, (((1,), (1,)), ((), ())), preferred_element_type=F32)


def _softmax_step(carry, s, v):
    m, l, acc = carry
    m_new = jnp.maximum(m, jnp.max(s, axis=1, keepdims=True))
    alpha = jnp.exp2(m - m_new)
    p = jnp.exp2(s - m_new)
    l = alpha * l + jnp.sum(p, axis=1, keepdims=True)
    acc = alpha * acc + jnp.dot(p.astype(BF16), v, preferred_element_type=F32)
    return m_new, l, acc


def _softmax_init(t, width):
    return (jnp.full((t, 1), NEG_INF, F32), jnp.zeros((t, 1), F32), jnp.zeros((t, width), F32))


def _paired_prefix(n_pairs, scores, consume, s_scr, init):
    def put(s):
        for slot, part in enumerate(s):
            s_scr[slot] = part

    def get():
        return tuple(s_scr[slot] for slot in range(s_scr.shape[0]))

    put(scores(0))

    def pair(i, carry):
        j = 2 * i
        s0 = get()
        s1 = scores(j + 1)
        carry = consume(carry, j, s0, False)
        put(scores(j + 2))
        return consume(carry, j + 1, s1, False)

    return lax.fori_loop(0, n_pairs, pair, init), get


def _wide_causal_sweep(p, scores, consume, s_scr, init):
    carry, get = _paired_prefix(p, scores, consume, s_scr, init)
    s_last = scores(2 * p + 1)
    carry = consume(carry, 2 * p, get(), True)
    return consume(carry, 2 * p + 1, s_last, True)


def _fox_kernel(q_ref, k_ref, v_ref, cc_ref, cr_ref, o_ref, s_scr):
    tq = q_ref.shape[0]
    tk = s_scr.shape[-1]
    h = pl.program_id(1)
    p = pl.program_id(2)
    q = q_ref[...]
    lane = lax.broadcasted_iota(jnp.int32, (tq, LANES), 1)
    cq = jnp.sum(jnp.where(lane == h, cc_ref[...], 0.0), axis=1, keepdims=True) * LOG2E

    def scores(j):
        start = pl.multiple_of(j * tk, tk)
        ck = cr_ref[:, pl.ds(start, tk)] * LOG2E
        return (_qk(q, k_ref[pl.ds(start, tk), :]) + cq - ck,)

    def consume(carry, j, s, diagonal):
        start = pl.multiple_of(j * tk, tk)
        s, = s
        if diagonal:
            row = lax.broadcasted_iota(jnp.int32, (tq, tk), 0)
            col = lax.broadcasted_iota(jnp.int32, (tq, tk), 1) + (j - 2 * p) * tk
            s = jnp.where(col <= row, s, NEG_INF)
        return _softmax_step(carry, s, v_ref[pl.ds(start, tk), :])

    _, l, acc = _wide_causal_sweep(p, scores, consume, s_scr, _softmax_init(tq, HEAD_DIM))
    o_ref[...] = (acc / l).astype(o_ref.dtype)


def _fox_attention(proj, cum_col, cum_row):
    b, s, _ = proj.shape
    t = 2 * ATT_T
    qb, kb, vb = QF_OFF // HEAD_DIM, KF_OFF // HEAD_DIM, VF_OFF // HEAD_DIM
    return pl.pallas_call(
        _fox_kernel,
        grid=(b, FOX_HEADS, s // t),
        in_specs=[pl.BlockSpec((None, t, HEAD_DIM), lambda bi, h, i: (bi, i, qb + h)),
                  pl.BlockSpec((None, s, HEAD_DIM), lambda bi, h, i: (bi, 0, kb + h)),
                  pl.BlockSpec((None, s, HEAD_DIM), lambda bi, h, i: (bi, 0, vb + h)),
                  pl.BlockSpec((None, t, LANES), lambda bi, h, i: (bi, i, 0)),
                  pl.BlockSpec((None, None, 1, s), lambda bi, h, i: (bi, h, 0, 0))],
        out_specs=pl.BlockSpec((None, t, HEAD_DIM), lambda bi, h, i: (bi, i, h)),
        out_shape=jax.ShapeDtypeStruct((b, s, FOX_WIDTH), BF16),
        scratch_shapes=[pltpu.VMEM((1, t, ATT_T), F32)],
        compiler_params=_params("arbitrary", "arbitrary", "arbitrary"),
        name="fox_attention",
    )(proj, proj, proj, cum_col, cum_row)


def _moba_kernel(q_ref, k_ref, v_ref, o_ref, kmean_ref, k_ext_ref, s_scr):
    t = q_ref.shape[0]
    tk = s_scr.shape[-1]
    blk = MOBA_BLOCK
    blk_shift = int(math.log2(blk))
    seq = k_ref.shape[0]
    nb = seq // blk
    nb_pad = kmean_ref.shape[0]
    p = pl.program_id(2)

    @pl.when(p == 0)
    def _():
        bidx = lax.broadcasted_iota(jnp.int32, (nb_pad, seq), 0)
        pos = lax.broadcasted_iota(jnp.int32, (nb_pad, seq), 1)
        member = jnp.where(lax.shift_right_logical(pos, blk_shift) == bidx, 1.0, 0.0)
        ksum = jnp.dot(member.astype(BF16), k_ref[...], preferred_element_type=F32)
        kmean_ref[...] = (ksum * (1.0 / blk)).astype(BF16)
        key = lax.broadcasted_iota(jnp.int32, (seq, LANES), 0)
        code = lax.broadcasted_iota(jnp.int32, (seq, LANES), 1)
        k_ext_ref[:, :HEAD_DIM] = k_ref[...]
        k_ext_ref[:, HEAD_DIM:] = jnp.where(lax.shift_right_logical(key, blk_shift) == code,
                                            1.0, 0.0).astype(k_ext_ref.dtype)

    q = q_ref[...]

    bj = lax.broadcasted_iota(jnp.int32, (nb_pad, t), 0)
    qcol = lax.broadcasted_iota(jnp.int32, (nb_pad, t), 1)
    own = (t // blk) * p + lax.shift_right_logical(qcol, blk_shift)
    past = bj < own
    gate = jnp.where(past, _qk(kmean_ref[...], q), NEG_INF)
    rank = jnp.zeros((nb_pad, t), F32)
    for jp in range(nb):
        c = gate[jp:jp + 1, :]
        tie = jnp.where(bj > jp, 1.0, 0.0)
        rank = rank + jnp.where(c > gate, 1.0, jnp.where(c == gate, tie, 0.0))
    hidden_t = jnp.where(bj == own, 0.0,
                         jnp.where(past, jnp.where(rank < float(MOBA_TOPK), 0.0, NEG_INF), NEG_INF))
    pad = jnp.zeros((LANES - nb_pad, t), F32)
    hidden = jnp.transpose(jnp.concatenate([hidden_t, pad], axis=0))
    q_ext = jnp.concatenate([q, hidden.astype(BF16)], axis=1)

    def scores(j):
        start = pl.multiple_of(j * tk, tk)
        return (_qk(q_ext, k_ext_ref[pl.ds(start, tk), :]),)

    def consume(carry, j, s, diagonal):
        start = pl.multiple_of(j * tk, tk)
        s, = s
        if diagonal:
            row = lax.broadcasted_iota(jnp.int32, (t, tk), 0)
            col = lax.broadcasted_iota(jnp.int32, (t, tk), 1) + (j - 2 * p) * tk
            s = jnp.where(col <= row, s, NEG_INF)
        return _softmax_step(carry, s, v_ref[pl.ds(start, tk), :])

    _, l, acc = _wide_causal_sweep(p, scores, consume, s_scr, _softmax_init(t, HEAD_DIM))
    o_ref[...] = (acc / l).astype(o_ref.dtype)


def _moba_attention(proj):
    b, s, _ = proj.shape
    tk = 2 * MOBA_BLOCK
    t = 2 * tk
    nb = s // MOBA_BLOCK
    nb_pad = -(-nb // 8) * 8
    assert s % t == 0 and nb_pad <= LANES and nb > MOBA_TOPK
    qb, kb, vb = QM_OFF // HEAD_DIM, KM_OFF // HEAD_DIM, VM_OFF // HEAD_DIM
    return pl.pallas_call(
        _moba_kernel,
        grid=(b, MOBA_HEADS, s // t),
        in_specs=[pl.BlockSpec((None, t, HEAD_DIM), lambda bi, h, i: (bi, i, qb + h)),
                  pl.BlockSpec((None, s, HEAD_DIM), lambda bi, h, i: (bi, 0, kb + h)),
                  pl.BlockSpec((None, s, HEAD_DIM), lambda bi, h, i: (bi, 0, vb + h))],
        out_specs=pl.BlockSpec((None, t, HEAD_DIM), lambda bi, h, i: (bi, i, h)),
        out_shape=jax.ShapeDtypeStruct((b, s, MOBA_WIDTH), BF16),
        scratch_shapes=[pltpu.VMEM((nb_pad, HEAD_DIM), BF16), pltpu.VMEM((s, HEAD_DIM + LANES), BF16),
                        pltpu.VMEM((1, t, tk), F32)],
        compiler_params=_params("arbitrary", "arbitrary", "arbitrary"),
        name="moba_attention",
    )(proj, proj, proj)


def _diff_kernel(q_ref, k_ref, v_ref, lq1_ref, lk1_ref, lq2_ref, lk2_ref, g_ref, o_ref, s_scr, *, lam_init):
    t = q_ref.shape[0]
    tk = s_scr.shape[-1]
    p = pl.program_id(2)
    lam = (jnp.exp(jnp.sum(lq1_ref[...] * lk1_ref[...], axis=1, keepdims=True))
           - jnp.exp(jnp.sum(lq2_ref[...] * lk2_ref[...], axis=1, keepdims=True))
           + lam_init)
    q1 = q_ref[:, :HEAD_DIM]
    q2 = q_ref[:, HEAD_DIM:]

    def scores(j):
        start = pl.multiple_of(j * tk, tk)
        return (_qk(q1, k_ref[pl.ds(start, tk), :HEAD_DIM]), _qk(q2, k_ref[pl.ds(start, tk), HEAD_DIM:]))

    def consume(carry, j, s, diagonal):
        start = pl.multiple_of(j * tk, tk)
        v = v_ref[pl.ds(start, tk), :]
        if diagonal:
            row = lax.broadcasted_iota(jnp.int32, (t, tk), 0)
            col = lax.broadcasted_iota(jnp.int32, (t, tk), 1) + (j - 2 * p) * tk
            s = tuple(jnp.where(col <= row, part, NEG_INF) for part in s)
        return tuple(_softmax_step(c, part, v) for c, part in zip(carry, s))

    init = (_softmax_init(t, DIFF_V_DIM), _softmax_init(t, DIFF_V_DIM))
    (_, l1, a1), (_, l2, a2) = _wide_causal_sweep(p, scores, consume, s_scr, init)
    o = a1 / l1 - lam * (a2 / l2)
    o = _rms(o, g_ref[...]) * (1.0 - lam_init)
    o_ref[...] = o.astype(o_ref.dtype)


def _diff_attention(proj, lq1, lk1, lq2, lk2, g_sub, lam_init):
    b, s, _ = proj.shape
    t = 2 * ATT_T
    w = DIFF_V_DIM
    qb, kb, vb = QD_OFF // w, KD_OFF // w, VD_OFF // w
    vec = lambda width: pl.BlockSpec((1, width), lambda bi, h, i: (0, 0))
    return pl.pallas_call(
        functools.partial(_diff_kernel, lam_init=lam_init),
        grid=(b, DIFF_HEADS, s // t),
        in_specs=[pl.BlockSpec((None, t, w), lambda bi, h, i: (bi, i, qb + h)),
                  pl.BlockSpec((None, s, w), lambda bi, h, i: (bi, 0, kb + h)),
                  pl.BlockSpec((None, s, w), lambda bi, h, i: (bi, 0, vb + h)),
                  vec(HEAD_DIM), vec(HEAD_DIM), vec(HEAD_DIM), vec(HEAD_DIM), vec(w)],
        out_specs=pl.BlockSpec((None, t, w), lambda bi, h, i: (bi, i, h)),
        out_shape=jax.ShapeDtypeStruct((b, s, DIFF_WIDTH), BF16),
        scratch_shapes=[pltpu.VMEM((2, t, ATT_T), F32)],
        compiler_params=_params("arbitrary", "arbitrary", "arbitrary"),
        name="diff_attention",
    )(proj, proj, proj, lq1.reshape(1, -1), lk1.reshape(1, -1), lq2.reshape(1, -1),
      lk2.reshape(1, -1), g_sub.reshape(1, -1))


def _outproj_kernel(oa_ref, ob_ref, oc_ref, wa_ref, wb_ref, wc_ref, x_ref, o_ref, w_scr):
    @pl.when(pl.program_id(1) == 0)
    def _():
        _cast_rows(w_scr, wa_ref, 0)
        _cast_rows(w_scr, wb_ref, MOBA_WIDTH)
        _cast_rows(w_scr, wc_ref, MOBA_WIDTH + FOX_WIDTH)

    acc = jnp.dot(oa_ref[...], w_scr[:MOBA_WIDTH, :], preferred_element_type=F32)
    acc = acc + jnp.dot(ob_ref[...], w_scr[MOBA_WIDTH:MOBA_WIDTH + FOX_WIDTH, :],
                        preferred_element_type=F32)
    acc = acc + jnp.dot(oc_ref[...], w_scr[MOBA_WIDTH + FOX_WIDTH:, :], preferred_element_type=F32)
    o_ref[...] = x_ref[...] + acc


def _outproj(o_a, o_b, o_c, w_out, layer, x, tm=1024, tn=512):
    m, d = x.shape
    assert MOBA_WIDTH == FOX_WIDTH and (MOBA_WIDTH + FOX_WIDTH) % DIFF_WIDTH == 0
    return pl.pallas_call(
        _outproj_kernel,
        grid=(d // tn, m // tm),
        in_specs=[pl.BlockSpec((tm, MOBA_WIDTH), lambda n, i: (i, 0)),
                  pl.BlockSpec((tm, FOX_WIDTH), lambda n, i: (i, 0)),
                  pl.BlockSpec((tm, DIFF_WIDTH), lambda n, i: (i, 0)),
                  pl.BlockSpec((None, MOBA_WIDTH, tn), lambda n, i: (layer, 0, n)),
                  pl.BlockSpec((None, FOX_WIDTH, tn), lambda n, i: (layer, 1, n)),
                  pl.BlockSpec((None, DIFF_WIDTH, tn),
                               lambda n, i: (layer, (MOBA_WIDTH + FOX_WIDTH) // DIFF_WIDTH, n)),
                  pl.BlockSpec((tm, tn), lambda n, i: (i, n))],
        out_specs=pl.BlockSpec((tm, tn), lambda n, i: (i, n)),
        out_shape=jax.ShapeDtypeStruct((m, d), F32),
        scratch_shapes=[pltpu.VMEM((MOBA_WIDTH + FOX_WIDTH + DIFF_WIDTH, tn), BF16)],
        compiler_params=_params("arbitrary", "arbitrary"),
        name="outproj",
    )(o_a, o_b, o_c, w_out, w_out, w_out, x)


def _ffn_up_kernel(h_ref, wg_hbm, wu_hbm, o_ref, stage, wg_scr, wu_scr, sems, *, layer, tn, n_full, tail):
    n = pl.program_id(0)
    first_row_tile = pl.program_id(1) == 0

    def fetch(tile, width):
        col = tile * tn if isinstance(tile, int) else pl.multiple_of(tile * tn, tn)
        return [pltpu.make_async_copy(w.at[layer, :, pl.ds(col, width)], stage.at[i, :, pl.ds(0, width)],
                                      sems.at[i])
                for i, w in enumerate((wg_hbm, wu_hbm))]

    def start(tile, width):
        for c in fetch(tile, width):
            c.start()

    def finish(tile, width):
        for c in fetch(tile, width):
            c.wait()
        for i, scr in enumerate((wg_scr, wu_scr)):
            for r in range(0, stage.shape[1], CAST_ROWS):
                scr[r:r + CAST_ROWS, :width] = stage[i, r:r + CAST_ROWS, :width].astype(BF16)
            if width < tn:
                scr[:, width:] = jnp.zeros((scr.shape[0], tn - width), BF16)

    @pl.when(first_row_tile & (n == 0))
    def _():
        start(0, tn)

    @pl.when(first_row_tile & (n < n_full))
    def _():
        finish(n, tn)

    @pl.when(first_row_tile & (n == n_full))
    def _():
        finish(n, tail)

    @pl.when(first_row_tile & (n + 1 < n_full))
    def _():
        start(n + 1, tn)

    @pl.when(first_row_tile & (n + 1 == n_full))
    def _():
        start(n + 1, tail)

    h = h_ref[...]
    g = jnp.dot(h, wg_scr[...], preferred_element_type=F32)
    u = jnp.dot(h, wu_scr[...], preferred_element_type=F32)
    o_ref[...] = (g / (1.0 + jnp.exp(-g)) * u).astype(o_ref.dtype)


def _ffn_up(h, w_gate, w_up, layer, tm=1024, tn=512):
    m, d = h.shape
    n_full, tail = divmod(FFN_DIM, tn)
    assert 0 < tail and tail % LANES == 0 and (n_full + 1) * tn == FFN_PAD and d % CAST_ROWS == 0
    return pl.pallas_call(
        functools.partial(_ffn_up_kernel, layer=layer, tn=tn, n_full=n_full, tail=tail),
        grid=(FFN_PAD // tn, m // tm),
        in_specs=[pl.BlockSpec((tm, d), lambda n, i: (i, 0)),
                  pl.BlockSpec(memory_space=pl.ANY),
                  pl.BlockSpec(memory_space=pl.ANY)],
        out_specs=pl.BlockSpec((tm, tn), lambda n, i: (i, n)),
        out_shape=jax.ShapeDtypeStruct((m, FFN_PAD), BF16),
        scratch_shapes=[pltpu.VMEM((2, d, tn), F32), pltpu.VMEM((d, tn), BF16), pltpu.VMEM((d, tn), BF16),
                        pltpu.SemaphoreType.DMA((2,))],
        compiler_params=_params("arbitrary", "arbitrary"),
        name="ffn_up",
    )(h, w_gate, w_up)


def _cast_pad_kernel(w_ref, o_ref, *, n_valid):
    @pl.when(pl.program_id(0) < n_valid)
    def _():
        o_ref[...] = w_ref[...].astype(o_ref.dtype)

    @pl.when(pl.program_id(0) >= n_valid)
    def _():
        o_ref[...] = jnp.zeros(o_ref.shape, o_ref.dtype)


def _cast_pad_rows(w, layer, rows_out, tr=256):
    _, rows, cols = w.shape
    n_valid = rows // tr
    assert rows % tr == 0 and rows_out % tr == 0
    return pl.pallas_call(
        functools.partial(_cast_pad_kernel, n_valid=n_valid),
        grid=(rows_out // tr,),
        in_specs=[pl.BlockSpec((None, tr, cols), lambda i: (layer, jnp.minimum(i, n_valid - 1), 0))],
        out_specs=pl.BlockSpec((tr, cols), lambda i: (i, 0)),
        out_shape=jax.ShapeDtypeStruct((rows_out, cols), BF16),
        compiler_params=_params("arbitrary"),
        name="cast_pad_rows",
    )(w)


def _ffn_down_kernel(a_ref, w_ref, x_ref, o_ref):
    k = pl.program_id(2)

    @pl.when(k == 0)
    def _():
        o_ref[...] = x_ref[...] + jnp.dot(a_ref[...], w_ref[...], preferred_element_type=F32)

    @pl.when(k > 0)
    def _():
        o_ref[...] += jnp.dot(a_ref[...], w_ref[...], preferred_element_type=F32)


def _ffn_down(a, wd, x, tm=1024, tn=1024, tk=2816):
    m, f = a.shape
    d = wd.shape[1]
    return pl.pallas_call(
        _ffn_down_kernel,
        grid=(m // tm, d // tn, f // tk),
        in_specs=[pl.BlockSpec((tm, tk), lambda i, j, k: (i, k)),
                  pl.BlockSpec((tk, tn), lambda i, j, k: (k, j)),
                  pl.BlockSpec((tm, tn), lambda i, j, k: (i, j))],
        out_specs=pl.BlockSpec((tm, tn), lambda i, j, k: (i, j)),
        out_shape=jax.ShapeDtypeStruct((m, d), F32),
        compiler_params=_params("arbitrary", "arbitrary", "arbitrary"),
        name="ffn_down",
    )(a, wd, x)


def _rotary_tables(seq):
    inv_freq = 1.0 / (ROPE_THETA ** (jnp.arange(0, HEAD_DIM, 2, dtype=F32) / HEAD_DIM))
    ang = jnp.arange(seq, dtype=F32)[:, None] * inv_freq[None, :]
    cos, sin = jnp.cos(ang), jnp.sin(ang)
    return jnp.concatenate([cos, cos], axis=-1), jnp.concatenate([-sin, sin], axis=-1)


def _column_gains(mq, mk, fq, fk, dq, dk):
    ones = lambda width: jnp.ones((width,), F32)
    rep = lambda g, width: jnp.tile(g.astype(F32), width // HEAD_DIM)
    qrep = lambda g, width: rep(g, width) * SCORE_SCALE
    return jnp.concatenate([
        qrep(mq, MOBA_WIDTH), rep(mk, MOBA_WIDTH), ones(MOBA_WIDTH),
        qrep(fq, FOX_WIDTH), rep(fk, FOX_WIDTH), ones(FOX_WIDTH),
        qrep(dq, DIFF_QK_WIDTH), rep(dk, DIFF_QK_WIDTH), ones(DIFF_WIDTH)]).reshape(1, PROJ_WIDTH)


def kernel(x, attn_norm, w_in, moba_q_norm, moba_k_norm, fox_q_norm, fox_k_norm, fox_forget_bias, diff_q_norm, diff_k_norm, diff_lambda_q1, diff_lambda_k1, diff_lambda_q2, diff_lambda_k2, diff_sub_norm, w_out, ffn_norm, w_gate, w_up, w_down):
    b, s, d = x.shape
    depth = w_in.shape[0]
    cos_full, sin_signed = _rotary_tables(s)
    x = x.reshape(b * s, d)
    w_in_t = _prep_w_in(w_in)
    for l in range(depth):
        gains = _column_gains(moba_q_norm[l], moba_k_norm[l], fox_q_norm[l], fox_k_norm[l],
                              diff_q_norm[l], diff_k_norm[l])
        bias = jnp.pad(fox_forget_bias[l].astype(F32), (0, LANES - FOX_HEADS)).reshape(1, LANES)

        h, flog = _rmsnorm_flog(x, attn_norm[l], w_in_t[l])
        proj = _inproj(h, w_in_t[l], gains, cos_full, sin_signed, s).reshape(b, s, PROJ_WIDTH)
        cum_col = _fox_prep(flog.reshape(b, s, LANES), bias)
        cum_row = jnp.transpose(cum_col[:, :, :FOX_HEADS], (0, 2, 1)).reshape(b, FOX_HEADS, 1, s)

        o_a = _moba_attention(proj)
        o_b = _fox_attention(proj, cum_col, cum_row)
        lam_init = 0.8 - 0.6 * math.exp(-0.3 * l)
        o_c = _diff_attention(proj, diff_lambda_q1[l], diff_lambda_k1[l], diff_lambda_q2[l],
                              diff_lambda_k2[l], diff_sub_norm[l], lam_init)
        x = _outproj(o_a.reshape(b * s, -1), o_b.reshape(b * s, -1), o_c.reshape(b * s, -1),
                     w_out, l, x)

        h = _rmsnorm(x, ffn_norm[l])
        a = _ffn_up(h, w_gate, w_up, l)
        x = _ffn_down(a, _cast_pad_rows(w_down, l, FFN_PAD), x)
    return x.reshape(b, s, d)
```

```python
import functools
import math

import jax
import jax.numpy as jnp
from jax import lax
from jax.experimental import pallas as pl
from jax.experimental.pallas import tpu as pltpu

D_MODEL = 4096
HEAD_DIM = 128
DIFF_V_DIM = 2 * HEAD_DIM
DIFF_HEADS = D_MODEL // (4 * DIFF_V_DIM)
MOBA_HEADS = (D_MODEL - DIFF_HEADS * DIFF_V_DIM) // (2 * HEAD_DIM)
FOX_HEADS = MOBA_HEADS
MOBA_WIDTH = MOBA_HEADS * HEAD_DIM
FOX_WIDTH = FOX_HEADS * HEAD_DIM
DIFF_QK_WIDTH = DIFF_HEADS * 2 * HEAD_DIM
DIFF_WIDTH = DIFF_HEADS * DIFF_V_DIM
FFN_DIM = -(-8 * D_MODEL // (3 * 256)) * 256
MOBA_BLOCK = 256
MOBA_TOPK = 3
ROPE_THETA = 10000.0
NORM_EPS = 1e-6
NEG_INF = -1e30
LOG2E = math.log2(math.e)

QM_OFF = 0
KM_OFF = QM_OFF + MOBA_WIDTH
VM_OFF = KM_OFF + MOBA_WIDTH
QF_OFF = VM_OFF + MOBA_WIDTH
KF_OFF = QF_OFF + FOX_WIDTH
VF_OFF = KF_OFF + FOX_WIDTH
FLOGIT_OFF = VF_OFF + FOX_WIDTH
QD_OFF = VF_OFF + FOX_WIDTH
KD_OFF = QD_OFF + DIFF_QK_WIDTH
VD_OFF = KD_OFF + DIFF_QK_WIDTH
PROJ_WIDTH = VD_OFF + DIFF_WIDTH

LANES = 128
FFN_PAD = -(-FFN_DIM // 1024) * 1024
VMEM_LIMIT = 56 * 1024 * 1024

IN_TN = 512
ATT_T = 512
CAST_ROWS = 512

F32 = jnp.float32
BF16 = jnp.bfloat16


def _params(*sem):
    return pltpu.CompilerParams(dimension_semantics=sem, vmem_limit_bytes=VMEM_LIMIT)


def _cast_rows(dst_ref, src_ref, dst_row0=0):
    rows = src_ref.shape[0]
    step = math.gcd(rows, CAST_ROWS)
    for r in range(0, rows, step):
        dst_ref[dst_row0 + r:dst_row0 + r + step, :] = src_ref[r:r + step, :].astype(BF16)


def _rms(x, g):
    ms = jnp.mean(x * x, axis=-1, keepdims=True)
    return x * lax.rsqrt(ms + NORM_EPS) * g


def _rmsnorm_kernel(x_ref, g_ref, o_ref):
    o_ref[...] = _rms(x_ref[...], g_ref[...]).astype(o_ref.dtype)


def _rmsnorm(x, g, tm=256):
    m, d = x.shape
    return pl.pallas_call(
        _rmsnorm_kernel,
        grid=(m // tm,),
        in_specs=[pl.BlockSpec((tm, d), lambda i: (i, 0)),
                  pl.BlockSpec((1, d), lambda i: (0, 0))],
        out_specs=pl.BlockSpec((tm, d), lambda i: (i, 0)),
        out_shape=jax.ShapeDtypeStruct((m, d), BF16),
        compiler_params=_params("arbitrary"),
        name="rmsnorm",
    )(x, g.reshape(1, d))


def _rmsnorm_flog_kernel(x_ref, g_ref, w_ref, o_ref, flog_ref):
    h = _rms(x_ref[...], g_ref[...]).astype(BF16)
    o_ref[...] = h
    flog_ref[...] = _nt_dot(h, w_ref[...])


def _rmsnorm_flog(x, g, wt, tm=256):
    m, d = x.shape
    return pl.pallas_call(
        _rmsnorm_flog_kernel,
        grid=(m // tm,),
        in_specs=[pl.BlockSpec((tm, d), lambda i: (i, 0)),
                  pl.BlockSpec((1, d), lambda i: (0, 0)),
                  pl.BlockSpec((LANES, d), lambda i: (PROJ_WIDTH // LANES, 0))],
        out_specs=[pl.BlockSpec((tm, d), lambda i: (i, 0)),
                   pl.BlockSpec((tm, LANES), lambda i: (i, 0))],
        out_shape=[jax.ShapeDtypeStruct((m, d), BF16),
                   jax.ShapeDtypeStruct((m, LANES), F32)],
        compiler_params=_params("arbitrary"),
        name="rmsnorm_flog",
    )(x, g.reshape(1, d), wt)


WT_TR = 256
WT_ROWS = PROJ_WIDTH + WT_TR


def _prep_w_in_kernel(w_ref, *o_refs):
    i = pl.program_id(0)
    row = lax.broadcasted_iota(jnp.int32, (WT_TR, 1), 0)
    keep = jnp.where(i < PROJ_WIDTH // WT_TR, WT_TR, FOX_HEADS)
    for layer, o_ref in enumerate(o_refs):
        o_ref[...] = jnp.where(row < keep, w_ref[:, layer, :], 0.0).astype(o_ref.dtype)


def _prep_w_in(w_in):
    depth, d, width = w_in.shape
    wt = jnp.transpose(w_in, (2, 0, 1))
    n_main = PROJ_WIDTH // WT_TR
    assert FLOGIT_OFF % WT_TR == 0 and width == PROJ_WIDTH + FOX_HEADS

    def src_row(i):
        shifted = jnp.where(i >= FLOGIT_OFF // WT_TR, i * WT_TR + FOX_HEADS, i * WT_TR)
        return jnp.where(i >= n_main, FLOGIT_OFF, shifted)

    return pl.pallas_call(
        _prep_w_in_kernel,
        grid=(n_main + 1,),
        in_specs=[pl.BlockSpec((pl.Element(WT_TR), pl.Element(depth), pl.Element(d)),
                               lambda i: (src_row(i), 0, 0))],
        out_specs=[pl.BlockSpec((WT_TR, d), lambda i: (i, 0))] * depth,
        out_shape=[jax.ShapeDtypeStruct((WT_ROWS, d), BF16)] * depth,
        compiler_params=_params("arbitrary"),
        name="prep_w_in",
    )(wt)


def _nt_dot(a, b):
    return lax.dot_general(a, b, (((1,), (1,)), ((), ())), preferred_element_type=F32)


def _tile_kind(n):
    def in_seg(off, width):
        return (n >= off // IN_TN) & (n < (off + width) // IN_TN)
    rot = (in_seg(QM_OFF, 2 * MOBA_WIDTH) | in_seg(QD_OFF, 2 * DIFF_QK_WIDTH))
    norm = in_seg(QF_OFF, 2 * FOX_WIDTH)
    return rot, norm


IN_TILES_PER_STEP = 2


def _inproj_kernel(h_ref, w_ref, gain_ref, cos_ref, sin_ref, o_ref, y_scr):
    for sub in range(IN_TILES_PER_STEP):
        rot, norm = _tile_kind(IN_TILES_PER_STEP * pl.program_id(1) + sub)
        c0 = sub * IN_TN
        tile = slice(c0, c0 + IN_TN)
        heads = [(slice(lo, lo + HEAD_DIM), slice(c0 + lo, c0 + lo + HEAD_DIM))
                 for lo in range(0, IN_TN, HEAD_DIM)]

        @pl.when(jnp.logical_not(rot | norm))
        def _():
            o_ref[:, tile] = _nt_dot(h_ref[...], w_ref[tile, :]).astype(o_ref.dtype)

        @pl.when(norm)
        def _():
            acc = _nt_dot(h_ref[...], w_ref[tile, :])
            for loc, glob in heads:
                o_ref[:, glob] = _rms(acc[:, loc], gain_ref[:, glob]).astype(o_ref.dtype)

        @pl.when(rot)
        def _():
            acc = _nt_dot(h_ref[...], w_ref[tile, :])
            for loc, glob in heads:
                y_scr[:, loc] = _rms(acc[:, loc], gain_ref[:, glob])

        @pl.when(rot)
        def _():
            for loc, glob in heads:
                y = y_scr[:, loc]
                y = y * cos_ref[...] + pltpu.roll(y, HEAD_DIM // 2, 1) * sin_ref[...]
                o_ref[:, glob] = y.astype(o_ref.dtype)


def _inproj(h, wt, gain, cos_full, sin_signed, seq, tm=1024):
    m, d = h.shape
    tn = IN_TILES_PER_STEP * IN_TN
    assert PROJ_WIDTH % tn == 0
    pos_tiles = seq // tm
    return pl.pallas_call(
        _inproj_kernel,
        grid=(m // tm, PROJ_WIDTH // tn),
        in_specs=[pl.BlockSpec((tm, d), lambda i, n: (i, 0)),
                  pl.BlockSpec((tn, d), lambda i, n: (n, 0)),
                  pl.BlockSpec((1, tn), lambda i, n: (0, n)),
                  pl.BlockSpec((tm, HEAD_DIM), lambda i, n: (i % pos_tiles, 0)),
                  pl.BlockSpec((tm, HEAD_DIM), lambda i, n: (i % pos_tiles, 0))],
        out_specs=pl.BlockSpec((tm, tn), lambda i, n: (i, n)),
        out_shape=jax.ShapeDtypeStruct((m, PROJ_WIDTH), BF16),
        scratch_shapes=[pltpu.VMEM((tm, IN_TN), F32)],
        compiler_params=_params("arbitrary", "arbitrary"),
        name="inproj",
    )(h, wt, gain, cos_full, sin_signed)


def _fox_prep_kernel(fl_ref, b_ref, o_ref, *, chunk):
    seq = fl_ref.shape[0]
    row = lax.broadcasted_iota(jnp.int32, (chunk, chunk), 0)
    col = lax.broadcasted_iota(jnp.int32, (chunk, chunk), 1)
    tri = jnp.where(row >= col, 1.0, 0.0).astype(F32)

    def body(i, carry):
        start = pl.multiple_of(i * chunk, chunk)
        z = fl_ref[pl.ds(start, chunk), :] + b_ref[...]
        log_f = -(jnp.maximum(-z, 0.0) + jnp.log1p(jnp.exp(-jnp.abs(z))))
        c = jnp.dot(tri, log_f, precision=lax.Precision.HIGHEST,
                    preferred_element_type=F32) + carry
        o_ref[pl.ds(start, chunk), :] = c
        return c[chunk - 1:chunk, :]

    lax.fori_loop(0, seq // chunk, body, jnp.zeros((1, LANES), F32))


def _fox_prep(flog, bias, chunk=256):
    b, s, _ = flog.shape
    return pl.pallas_call(
        functools.partial(_fox_prep_kernel, chunk=chunk),
        grid=(b,),
        in_specs=[pl.BlockSpec((None, s, LANES), lambda i: (i, 0, 0)),
                  pl.BlockSpec((1, LANES), lambda i: (0, 0))],
        out_specs=pl.BlockSpec((None, s, LANES), lambda i: (i, 0, 0)),
        out_shape=jax.ShapeDtypeStruct((b, s, LANES), F32),
        compiler_params=_params("arbitrary"),
        name="fox_prep",
    )(flog, bias)


SCORE_SCALE = HEAD_DIM ** -0.5 * LOG2E


def _qk(q, k):
    return lax.dot_general(q, k, (((1,), (1,)), ((), ())), preferred_element_type=F32)


def _softmax_step(carry, s, v):
    m, l, acc = carry
    m_new = jnp.maximum(m, jnp.max(s, axis=1, keepdims=True))
    alpha = jnp.exp2(m - m_new)
    p = jnp.exp2(s - m_new)
    l = alpha * l + jnp.sum(p, axis=1, keepdims=True)
    acc = alpha * acc + jnp.dot(p.astype(BF16), v, preferred_element_type=F32)
    return m_new, l, acc


def _softmax_init(t, width):
    return (jnp.full((t, 1), NEG_INF, F32), jnp.zeros((t, 1), F32), jnp.zeros((t, width), F32))


def _paired_prefix(n_pairs, scores, consume, s_scr, init):
    def put(s):
        for slot, part in enumerate(s):
            s_scr[slot] = part

    def get():
        return tuple(s_scr[slot] for slot in range(s_scr.shape[0]))

    put(scores(0))

    def pair(i, carry):
        j = 2 * i
        s0 = get()
        s1 = scores(j + 1)
        carry = consume(carry, j, s0, False)
        put(scores(j + 2))
        return consume(carry, j + 1, s1, False)

    return lax.fori_loop(0, n_pairs, pair, init), get


def _causal_mask(s, tq, col0):
    row = lax.broadcasted_iota(jnp.int32, s.shape, 0) + (tq - s.shape[0])
    col = lax.broadcasted_iota(jnp.int32, s.shape, 1) + col0
    return jnp.where(col <= row, s, NEG_INF)


def _wide_causal_sweep(p, scores, consume, s_scr, init):
    tk = s_scr.shape[-1]
    lower = slice(tk, 2 * tk)
    carry, get = _paired_prefix(p, scores, consume, s_scr, init)
    s_last = scores(2 * p + 1, lower)
    carry = consume(carry, 2 * p, get(), True)
    low = consume(jax.tree.map(lambda x: x[lower], carry), 2 * p + 1, s_last, True)
    return jax.tree.map(lambda full, part: jnp.concatenate([full[:tk], part], axis=0), carry, low)


def _fox_kernel(q_ref, k_ref, v_ref, cc_ref, cr_ref, o_ref, s_scr):
    tq = q_ref.shape[0]
    tk = s_scr.shape[-1]
    h = pl.program_id(1)
    p = pl.program_id(2)
    q = q_ref[...]
    lane = lax.broadcasted_iota(jnp.int32, (tq, LANES), 1)
    cq = jnp.sum(jnp.where(lane == h, cc_ref[...], 0.0), axis=1, keepdims=True) * LOG2E

    def scores(j, rows=slice(None)):
        start = pl.multiple_of(j * tk, tk)
        ck = cr_ref[:, pl.ds(start, tk)] * LOG2E
        return (_qk(q[rows], k_ref[pl.ds(start, tk), :]) + cq[rows] - ck,)

    def consume(carry, j, s, diagonal):
        start = pl.multiple_of(j * tk, tk)
        s, = s
        if diagonal:
            s = _causal_mask(s, tq, (j - 2 * p) * tk)
        return _softmax_step(carry, s, v_ref[pl.ds(start, tk), :])

    _, l, acc = _wide_causal_sweep(p, scores, consume, s_scr, _softmax_init(tq, HEAD_DIM))
    o_ref[...] = (acc / l).astype(o_ref.dtype)


def _fox_attention(proj, cum_col, cum_row):
    b, s, _ = proj.shape
    t = 2 * ATT_T
    qb, kb, vb = QF_OFF // HEAD_DIM, KF_OFF // HEAD_DIM, VF_OFF // HEAD_DIM
    return pl.pallas_call(
        _fox_kernel,
        grid=(b, FOX_HEADS, s // t),
        in_specs=[pl.BlockSpec((None, t, HEAD_DIM), lambda bi, h, i: (bi, i, qb + h)),
                  pl.BlockSpec((None, s, HEAD_DIM), lambda bi, h, i: (bi, 0, kb + h)),
                  pl.BlockSpec((None, s, HEAD_DIM), lambda bi, h, i: (bi, 0, vb + h)),
                  pl.BlockSpec((None, t, LANES), lambda bi, h, i: (bi, i, 0)),
                  pl.BlockSpec((None, None, 1, s), lambda bi, h, i: (bi, h, 0, 0))],
        out_specs=pl.BlockSpec((None, t, HEAD_DIM), lambda bi, h, i: (bi, i, h)),
        out_shape=jax.ShapeDtypeStruct((b, s, FOX_WIDTH), BF16),
        scratch_shapes=[pltpu.VMEM((1, t, ATT_T), F32)],
        compiler_params=_params("arbitrary", "arbitrary", "arbitrary"),
        name="fox_attention",
    )(proj, proj, proj, cum_col, cum_row)


def _moba_kernel(q_ref, k_ref, v_ref, o_ref, kmean_ref, k_ext_ref, s_scr):
    t = q_ref.shape[0]
    tk = s_scr.shape[-1]
    blk = MOBA_BLOCK
    blk_shift = int(math.log2(blk))
    seq = k_ref.shape[0]
    nb = seq // blk
    nb_pad = kmean_ref.shape[0]
    p = pl.program_id(2)

    @pl.when(p == 0)
    def _():
        bidx = lax.broadcasted_iota(jnp.int32, (nb_pad, seq), 0)
        pos = lax.broadcasted_iota(jnp.int32, (nb_pad, seq), 1)
        member = jnp.where(lax.shift_right_logical(pos, blk_shift) == bidx, 1.0, 0.0)
        ksum = jnp.dot(member.astype(BF16), k_ref[...], preferred_element_type=F32)
        kmean_ref[...] = (ksum * (1.0 / blk)).astype(BF16)
        key = lax.broadcasted_iota(jnp.int32, (seq, LANES), 0)
        code = lax.broadcasted_iota(jnp.int32, (seq, LANES), 1)
        k_ext_ref[:, :HEAD_DIM] = k_ref[...]
        k_ext_ref[:, HEAD_DIM:] = jnp.where(lax.shift_right_logical(key, blk_shift) == code,
                                            1.0, 0.0).astype(k_ext_ref.dtype)

    q = q_ref[...]

    bj = lax.broadcasted_iota(jnp.int32, (nb_pad, t), 0)
    qcol = lax.broadcasted_iota(jnp.int32, (nb_pad, t), 1)
    own = (t // blk) * p + lax.shift_right_logical(qcol, blk_shift)
    past = bj < own
    gate = jnp.where(past, _qk(kmean_ref[...], q), NEG_INF)
    rank = jnp.zeros((nb_pad, t), F32)
    for jp in range(nb):
        c = gate[jp:jp + 1, :]
        tie = jnp.where(bj > jp, 1.0, 0.0)
        rank = rank + jnp.where(c > gate, 1.0, jnp.where(c == gate, tie, 0.0))
    hidden_t = jnp.where(bj == own, 0.0,
                         jnp.where(past, jnp.where(rank < float(MOBA_TOPK), 0.0, NEG_INF), NEG_INF))
    pad = jnp.zeros((LANES - nb_pad, t), F32)
    hidden = jnp.transpose(jnp.concatenate([hidden_t, pad], axis=0))
    q_ext = jnp.concatenate([q, hidden.astype(BF16)], axis=1)

    def scores(j, rows=slice(None)):
        start = pl.multiple_of(j * tk, tk)
        return (_qk(q_ext[rows], k_ext_ref[pl.ds(start, tk), :]),)

    def consume(carry, j, s, diagonal):
        start = pl.multiple_of(j * tk, tk)
        s, = s
        if diagonal:
            s = _causal_mask(s, t, (j - 2 * p) * tk)
        return _softmax_step(carry, s, v_ref[pl.ds(start, tk), :])

    _, l, acc = _wide_causal_sweep(p, scores, consume, s_scr, _softmax_init(t, HEAD_DIM))
    o_ref[...] = (acc / l).astype(o_ref.dtype)


def _moba_attention(proj):
    b, s, _ = proj.shape
    tk = 2 * MOBA_BLOCK
    t = 2 * tk
    nb = s // MOBA_BLOCK
    nb_pad = -(-nb // 8) * 8
    assert s % t == 0 and nb_pad <= LANES and nb > MOBA_TOPK
    qb, kb, vb = QM_OFF // HEAD_DIM, KM_OFF // HEAD_DIM, VM_OFF // HEAD_DIM
    return pl.pallas_call(
        _moba_kernel,
        grid=(b, MOBA_HEADS, s // t),
        in_specs=[pl.BlockSpec((None, t, HEAD_DIM), lambda bi, h, i: (bi, i, qb + h)),
                  pl.BlockSpec((None, s, HEAD_DIM), lambda bi, h, i: (bi, 0, kb + h)),
                  pl.BlockSpec((None, s, HEAD_DIM), lambda bi, h, i: (bi, 0, vb + h))],
        out_specs=pl.BlockSpec((None, t, HEAD_DIM), lambda bi, h, i: (bi, i, h)),
        out_shape=jax.ShapeDtypeStruct((b, s, MOBA_WIDTH), BF16),
        scratch_shapes=[pltpu.VMEM((nb_pad, HEAD_DIM), BF16), pltpu.VMEM((s, HEAD_DIM + LANES), BF16),
                        pltpu.VMEM((1, t, tk), F32)],
        compiler_params=_params("arbitrary", "arbitrary", "arbitrary"),
        name="moba_attention",
    )(proj, proj, proj)


def _diff_kernel(q_ref, k_ref, v_ref, lq1_ref, lk1_ref, lq2_ref, lk2_ref, g_ref, o_ref, s_scr, *, lam_init):
    t = q_ref.shape[0]
    tk = s_scr.shape[-1]
    p = pl.program_id(2)
    lam = (jnp.exp(jnp.sum(lq1_ref[...] * lk1_ref[...], axis=1, keepdims=True))
           - jnp.exp(jnp.sum(lq2_ref[...] * lk2_ref[...], axis=1, keepdims=True))
           + lam_init)
    q1 = q_ref[:, :HEAD_DIM]
    q2 = q_ref[:, HEAD_DIM:]

    def scores(j, rows=slice(None)):
        start = pl.multiple_of(j * tk, tk)
        return (_qk(q1[rows], k_ref[pl.ds(start, tk), :HEAD_DIM]),
                _qk(q2[rows], k_ref[pl.ds(start, tk), HEAD_DIM:]))

    def consume(carry, j, s, diagonal):
        start = pl.multiple_of(j * tk, tk)
        v = v_ref[pl.ds(start, tk), :]
        if diagonal:
            s = tuple(_causal_mask(part, t, (j - 2 * p) * tk) for part in s)
        return tuple(_softmax_step(c, part, v) for c, part in zip(carry, s))

    init = (_softmax_init(t, DIFF_V_DIM), _softmax_init(t, DIFF_V_DIM))
    (_, l1, a1), (_, l2, a2) = _wide_causal_sweep(p, scores, consume, s_scr, init)
    o = a1 / l1 - lam * (a2 / l2)
    o = _rms(o, g_ref[...]) * (1.0 - lam_init)
    o_ref[...] = o.astype(o_ref.dtype)


def _diff_attention(proj, lq1, lk1, lq2, lk2, g_sub, lam_init):
    b, s, _ = proj.shape
    t = 2 * ATT_T
    w = DIFF_V_DIM
    qb, kb, vb = QD_OFF // w, KD_OFF // w, VD_OFF // w
    vec = lambda width: pl.BlockSpec((1, width), lambda bi, h, i: (0, 0))
    return pl.pallas_call(
        functools.partial(_diff_kernel, lam_init=lam_init),
        grid=(b, DIFF_HEADS, s // t),
        in_specs=[pl.BlockSpec((None, t, w), lambda bi, h, i: (bi, i, qb + h)),
                  pl.BlockSpec((None, s, w), lambda bi, h, i: (bi, 0, kb + h)),
                  pl.BlockSpec((None, s, w), lambda bi, h, i: (bi, 0, vb + h)),
                  vec(HEAD_DIM), vec(HEAD_DIM), vec(HEAD_DIM), vec(HEAD_DIM), vec(w)],
        out_specs=pl.BlockSpec((None, t, w), lambda bi, h, i: (bi, i, h)),
        out_shape=jax.ShapeDtypeStruct((b, s, DIFF_WIDTH), BF16),
        scratch_shapes=[pltpu.VMEM((2, t, ATT_T), F32)],
        compiler_params=_params("arbitrary", "arbitrary", "arbitrary"),
        name="diff_attention",
    )(proj, proj, proj, lq1.reshape(1, -1), lk1.reshape(1, -1), lq2.reshape(1, -1),
      lk2.reshape(1, -1), g_sub.reshape(1, -1))


def _outproj_kernel(oa_ref, ob_ref, oc_ref, wa_ref, wb_ref, wc_ref, x_ref, o_ref, w_scr):
    @pl.when(pl.program_id(1) == 0)
    def _():
        _cast_rows(w_scr, wa_ref, 0)
        _cast_rows(w_scr, wb_ref, MOBA_WIDTH)
        _cast_rows(w_scr, wc_ref, MOBA_WIDTH + FOX_WIDTH)

    acc = jnp.dot(oa_ref[...], w_scr[:MOBA_WIDTH, :], preferred_element_type=F32)
    acc = acc + jnp.dot(ob_ref[...], w_scr[MOBA_WIDTH:MOBA_WIDTH + FOX_WIDTH, :],
                        preferred_element_type=F32)
    acc = acc + jnp.dot(oc_ref[...], w_scr[MOBA_WIDTH + FOX_WIDTH:, :], preferred_element_type=F32)
    o_ref[...] = x_ref[...] + acc


def _outproj(o_a, o_b, o_c, w_out, layer, x, tm=1024, tn=512):
    m, d = x.shape
    assert MOBA_WIDTH == FOX_WIDTH and (MOBA_WIDTH + FOX_WIDTH) % DIFF_WIDTH == 0
    return pl.pallas_call(
        _outproj_kernel,
        grid=(d // tn, m // tm),
        in_specs=[pl.BlockSpec((tm, MOBA_WIDTH), lambda n, i: (i, 0)),
                  pl.BlockSpec((tm, FOX_WIDTH), lambda n, i: (i, 0)),
                  pl.BlockSpec((tm, DIFF_WIDTH), lambda n, i: (i, 0)),
                  pl.BlockSpec((None, MOBA_WIDTH, tn), lambda n, i: (layer, 0, n)),
                  pl.BlockSpec((None, FOX_WIDTH, tn), lambda n, i: (layer, 1, n)),
                  pl.BlockSpec((None, DIFF_WIDTH, tn),
                               lambda n, i: (layer, (MOBA_WIDTH + FOX_WIDTH) // DIFF_WIDTH, n)),
                  pl.BlockSpec((tm, tn), lambda n, i: (i, n))],
        out_specs=pl.BlockSpec((tm, tn), lambda n, i: (i, n)),
        out_shape=jax.ShapeDtypeStruct((m, d), F32),
        scratch_shapes=[pltpu.VMEM((MOBA_WIDTH + FOX_WIDTH + DIFF_WIDTH, tn), BF16)],
        compiler_params=_params("arbitrary", "arbitrary"),
        name="outproj",
    )(o_a, o_b, o_c, w_out, w_out, w_out, x)


def _ffn_up_kernel(h_ref, wg_hbm, wu_hbm, o_ref, stage, wg_scr, wu_scr, sems, *, layer, tn, n_full, tail):
    n = pl.program_id(0)
    first_row_tile = pl.program_id(1) == 0

    def fetch(tile, width):
        col = tile * tn if isinstance(tile, int) else pl.multiple_of(tile * tn, tn)
        return [pltpu.make_async_copy(w.at[layer, :, pl.ds(col, width)], stage.at[i, :, pl.ds(0, width)],
                                      sems.at[i])
                for i, w in enumerate((wg_hbm, wu_hbm))]

    def start(tile, width):
        for c in fetch(tile, width):
            c.start()

    def finish(tile, width):
        for c in fetch(tile, width):
            c.wait()
        for i, scr in enumerate((wg_scr, wu_scr)):
            for r in range(0, stage.shape[1], CAST_ROWS):
                scr[r:r + CAST_ROWS, :width] = stage[i, r:r + CAST_ROWS, :width].astype(BF16)
            if width < tn:
                scr[:, width:] = jnp.zeros((scr.shape[0], tn - width), BF16)

    @pl.when(first_row_tile & (n == 0))
    def _():
        start(0, tn)

    @pl.when(first_row_tile & (n < n_full))
    def _():
        finish(n, tn)

    @pl.when(first_row_tile & (n == n_full))
    def _():
        finish(n, tail)

    @pl.when(first_row_tile & (n + 1 < n_full))
    def _():
        start(n + 1, tn)

    @pl.when(first_row_tile & (n + 1 == n_full))
    def _():
        start(n + 1, tail)

    h = h_ref[...]
    g = jnp.dot(h, wg_scr[...], preferred_element_type=F32)
    u = jnp.dot(h, wu_scr[...], preferred_element_type=F32)
    o_ref[...] = (g / (1.0 + jnp.exp(-g)) * u).astype(o_ref.dtype)


def _ffn_up(h, w_gate, w_up, layer, tm=1024, tn=512):
    m, d = h.shape
    n_full, tail = divmod(FFN_DIM, tn)
    assert 0 < tail and tail % LANES == 0 and (n_full + 1) * tn == FFN_PAD and d % CAST_ROWS == 0
    return pl.pallas_call(
        functools.partial(_ffn_up_kernel, layer=layer, tn=tn, n_full=n_full, tail=tail),
        grid=(FFN_PAD // tn, m // tm),
        in_specs=[pl.BlockSpec((tm, d), lambda n, i: (i, 0)),
                  pl.BlockSpec(memory_space=pl.ANY),
                  pl.BlockSpec(memory_space=pl.ANY)],
        out_specs=pl.BlockSpec((tm, tn), lambda n, i: (i, n)),
        out_shape=jax.ShapeDtypeStruct((m, FFN_PAD), BF16),
        scratch_shapes=[pltpu.VMEM((2, d, tn), F32), pltpu.VMEM((d, tn), BF16), pltpu.VMEM((d, tn), BF16),
                        pltpu.SemaphoreType.DMA((2,))],
        compiler_params=_params("arbitrary", "arbitrary"),
        name="ffn_up",
    )(h, w_gate, w_up)


def _cast_pad_kernel(w_ref, o_ref, *, n_valid):
    @pl.when(pl.program_id(0) < n_valid)
    def _():
        o_ref[...] = w_ref[...].astype(o_ref.dtype)

    @pl.when(pl.program_id(0) >= n_valid)
    def _():
        o_ref[...] = jnp.zeros(o_ref.shape, o_ref.dtype)


def _cast_pad_rows(w, layer, rows_out, tr=256):
    _, rows, cols = w.shape
    n_valid = rows // tr
    assert rows % tr == 0 and rows_out % tr == 0
    return pl.pallas_call(
        functools.partial(_cast_pad_kernel, n_valid=n_valid),
        grid=(rows_out // tr,),
        in_specs=[pl.BlockSpec((None, tr, cols), lambda i: (layer, jnp.minimum(i, n_valid - 1), 0))],
        out_specs=pl.BlockSpec((tr, cols), lambda i: (i, 0)),
        out_shape=jax.ShapeDtypeStruct((rows_out, cols), BF16),
        compiler_params=_params("arbitrary"),
        name="cast_pad_rows",
    )(w)


def _ffn_down_kernel(a_ref, w_ref, x_ref, o_ref):
    k = pl.program_id(2)

    @pl.when(k == 0)
    def _():
        o_ref[...] = x_ref[...] + jnp.dot(a_ref[...], w_ref[...], preferred_element_type=F32)

    @pl.when(k > 0)
    def _():
        o_ref[...] += jnp.dot(a_ref[...], w_ref[...], preferred_element_type=F32)


def _ffn_down(a, wd, x, tm=1024, tn=1024, tk=2816):
    m, f = a.shape
    d = wd.shape[1]
    return pl.pallas_call(
        _ffn_down_kernel,
        grid=(m // tm, d // tn, f // tk),
        in_specs=[pl.BlockSpec((tm, tk), lambda i, j, k: (i, k)),
                  pl.BlockSpec((tk, tn), lambda i, j, k: (k, j)),
                  pl.BlockSpec((tm, tn), lambda i, j, k: (i, j))],
        out_specs=pl.BlockSpec((tm, tn), lambda i, j, k: (i, j)),
        out_shape=jax.ShapeDtypeStruct((m, d), F32),
        compiler_params=_params("arbitrary", "arbitrary", "arbitrary"),
        name="ffn_down",
    )(a, wd, x)


def _rotary_tables(seq):
    inv_freq = 1.0 / (ROPE_THETA ** (jnp.arange(0, HEAD_DIM, 2, dtype=F32) / HEAD_DIM))
    ang = jnp.arange(seq, dtype=F32)[:, None] * inv_freq[None, :]
    cos, sin = jnp.cos(ang), jnp.sin(ang)
    return jnp.concatenate([cos, cos], axis=-1), jnp.concatenate([-sin, sin], axis=-1)


def _column_gains(mq, mk, fq, fk, dq, dk):
    ones = lambda width: jnp.ones((width,), F32)
    rep = lambda g, width: jnp.tile(g.astype(F32), width // HEAD_DIM)
    qrep = lambda g, width: rep(g, width) * SCORE_SCALE
    return jnp.concatenate([
        qrep(mq, MOBA_WIDTH), rep(mk, MOBA_WIDTH), ones(MOBA_WIDTH),
        qrep(fq, FOX_WIDTH), rep(fk, FOX_WIDTH), ones(FOX_WIDTH),
        qrep(dq, DIFF_QK_WIDTH), rep(dk, DIFF_QK_WIDTH), ones(DIFF_WIDTH)]).reshape(1, PROJ_WIDTH)


def kernel(x, attn_norm, w_in, moba_q_norm, moba_k_norm, fox_q_norm, fox_k_norm, fox_forget_bias, diff_q_norm, diff_k_norm, diff_lambda_q1, diff_lambda_k1, diff_lambda_q2, diff_lambda_k2, diff_sub_norm, w_out, ffn_norm, w_gate, w_up, w_down):
    b, s, d = x.shape
    depth = w_in.shape[0]
    cos_full, sin_signed = _rotary_tables(s)
    x = x.reshape(b * s, d)
    w_in_t = _prep_w_in(w_in)
    for l in range(depth):
        gains = _column_gains(moba_q_norm[l], moba_k_norm[l], fox_q_norm[l], fox_k_norm[l],
                              diff_q_norm[l], diff_k_norm[l])
        bias = jnp.pad(fox_forget_bias[l].astype(F32), (0, LANES - FOX_HEADS)).reshape(1, LANES)

        h, flog = _rmsnorm_flog(x, attn_norm[l], w_in_t[l])
        proj = _inproj(h, w_in_t[l], gains, cos_full, sin_signed, s).reshape(b, s, PROJ_WIDTH)
        cum_col = _fox_prep(flog.reshape(b, s, LANES), bias)
        cum_row = jnp.transpose(cum_col[:, :, :FOX_HEADS], (0, 2, 1)).reshape(b, FOX_HEADS, 1, s)

        o_a = _moba_attention(proj)
        o_b = _fox_attention(proj, cum_col, cum_row)
        lam_init = 0.8 - 0.6 * math.exp(-0.3 * l)
        o_c = _diff_attention(proj, diff_lambda_q1[l], diff_lambda_k1[l], diff_lambda_q2[l],
                              diff_lambda_k2[l], diff_sub_norm[l], lam_init)
        x = _outproj(o_a.reshape(b * s, -1), o_b.reshape(b * s, -1), o_c.reshape(b * s, -1),
                     w_out, l, x)

        h = _rmsnorm(x, ffn_norm[l])
        a = _ffn_up(h, w_gate, w_up, l)
        x = _ffn_down(a, _cast_pad_rows(w_down, l, FFN_PAD), x)
    return x.reshape(b, s, d)
```
